```python
import jax, jax.numpy as jnp
from jax import lax
import numpy as np

D_MODEL = 1024
BATCH = 16
SEQ = 4096
DEPTH = 2

MIX_WIDTH = 1024
SWA_HEADS = 8
SWA_KV_HEADS = 2
SWA_HEAD_DIM = 64
SWA_GROUP = SWA_HEADS // SWA_KV_HEADS
SWA_WIDTH = SWA_HEADS * SWA_HEAD_DIM
SWA_KV_WIDTH = SWA_KV_HEADS * SWA_HEAD_DIM
WINDOW = 128
DN_HEADS = 4
DN_HEAD_DIM = 64
DN_WIDTH = DN_HEADS * DN_HEAD_DIM
DN_CONV = 4
DN_CHUNK = 64
N_MEM = 256
MEM_HEADS = 4
MEM_HEAD_DIM = 64
MEM_WIDTH = MEM_HEADS * MEM_HEAD_DIM

EPS = 1e-6
IN_SIZES = (SWA_WIDTH, SWA_KV_WIDTH, SWA_KV_WIDTH,
            DN_WIDTH, DN_WIDTH, DN_WIDTH, DN_HEADS, DN_HEADS,
            MEM_WIDTH, MIX_WIDTH)
IN_WIDTH = SWA_WIDTH + 2 * SWA_KV_WIDTH + 3 * DN_WIDTH + 2 * DN_HEADS + MEM_WIDTH + MIX_WIDTH

kernel_name = "hybrid_swa_sink_gdn_memory_parallel_heads"


def _split_points(sizes):
    pts, acc = [], 0
    for s in sizes[:-1]:
        acc += s
        pts.append(acc)
    return pts


def rms_norm(x, g):
    xf = x.astype(jnp.float32)
    y = xf * lax.rsqrt(jnp.mean(xf * xf, axis=-1, keepdims=True) + EPS)
    return (y * g.astype(jnp.float32)).astype(x.dtype)


def l2_norm(x):
    xf = x.astype(jnp.float32)
    return xf * lax.rsqrt(jnp.sum(xf * xf, axis=-1, keepdims=True) + EPS)


def swa_sink_attention(q, k, v, sinks):
    B, S = q.shape[0], q.shape[1]
    nb = S // WINDOW
    qb = q.reshape(B, nb, WINDOW, SWA_KV_HEADS, SWA_GROUP, SWA_HEAD_DIM)
    kb = k.reshape(B, nb, WINDOW, SWA_KV_HEADS, SWA_HEAD_DIM)
    vb = v.reshape(B, nb, WINDOW, SWA_KV_HEADS, SWA_HEAD_DIM)

    def with_prev(t):
        prev = jnp.pad(t[:, :-1], ((0, 0), (1, 0), (0, 0), (0, 0), (0, 0)))
        return jnp.concatenate([prev, t], axis=2)

    kk, vv = with_prev(kb), with_prev(vb)
    scale = SWA_HEAD_DIM ** -0.5
    s = jnp.einsum("bnqhgd,bnkhd->bnhgqk", qb, kk).astype(jnp.float32) * scale
    qi = jnp.arange(WINDOW)[:, None]
    kj = jnp.arange(2 * WINDOW)[None, :]
    band = (kj > qi) & (kj <= qi + WINDOW)
    blk = jnp.arange(nb)[:, None, None]
    valid = band[None] & ((kj[None] >= WINDOW) | (blk > 0))
    s = jnp.where(valid[None, :, None, None], s, -jnp.inf)
    sink = sinks.astype(jnp.float32).reshape(1, 1, SWA_KV_HEADS, SWA_GROUP, 1, 1)
    m = jnp.maximum(jnp.max(s, axis=-1, keepdims=True), sink)
    p = jnp.exp(s - m)
    denom = jnp.sum(p, axis=-1, keepdims=True) + jnp.exp(sink - m)
    o = jnp.einsum("bnhgqk,bnkhd->bnqhgd", (p / denom).astype(v.dtype), vv)
    return o.reshape(B, S, SWA_WIDTH)


def causal_depthwise_conv(x, w):
    K, C = w.shape
    return lax.conv_general_dilated(
        x, w[:, None, :].astype(x.dtype), window_strides=(1,), padding=[(K - 1, 0)],
        dimension_numbers=("NWC", "WIO", "NWC"), feature_group_count=C)


def gated_delta_rule(q, k, v, g, beta):
    B, S, H, dk = q.shape
    dv = v.shape[-1]
    C = DN_CHUNK
    N = S // C

    def chunks(t):
        return t.reshape(B, N, C, H, -1).transpose(0, 3, 1, 2, 4)

    q = chunks(q) * (dk ** -0.5)
    k = chunks(k)
    v = chunks(v)
    g = jnp.cumsum(g.reshape(B, N, C, H).transpose(0, 3, 1, 2), axis=-1)
    beta = beta.reshape(B, N, C, H).transpose(0, 3, 1, 2)
    causal = jnp.tril(jnp.ones((C, C), dtype=bool))
    strict = jnp.tril(jnp.ones((C, C), dtype=bool), -1)
    decay = jnp.exp(jnp.where(causal, g[..., :, None] - g[..., None, :], -jnp.inf))
    k_beta = k * beta[..., None]
    v_beta = v * beta[..., None]
    L = jnp.where(strict, jnp.einsum("bhncd,bhnsd->bhncs", k_beta, k) * decay, 0.0)
    a = L + jnp.eye(C, dtype=jnp.float32)
    rhs = jnp.concatenate([v_beta, k_beta * jnp.exp(g)[..., None]], axis=-1)
    sol = lax.linalg.triangular_solve(a, rhs, left_side=True, lower=True, unit_diagonal=True)
    u, w = sol[..., :dv], sol[..., dv:]

    def step(state, inp):
        q_i, k_i, u_i, w_i, g_i, dec_i = inp
        v_new = u_i - jnp.einsum("bhck,bhkv->bhcv", w_i, state)
        intra = jnp.einsum("bhck,bhsk->bhcs", q_i, k_i) * dec_i
        o_i = (jnp.einsum("bhck,bhkv->bhcv", q_i * jnp.exp(g_i)[..., None], state)
               + jnp.einsum("bhcs,bhsv->bhcv", intra, v_new))
        g_last = g_i[..., -1:]
        state = (state * jnp.exp(g_last)[..., None]
                 + jnp.einsum("bhck,bhcv->bhkv", k_i * jnp.exp(g_last - g_i)[..., None], v_new))
        return state, o_i

    xs = tuple(jnp.moveaxis(t, 2, 0) for t in (q, k, u, w, g, decay))
    s0 = jnp.zeros((B, H, dk, dv), jnp.float32)
    _, o = lax.scan(step, s0, xs)
    return o.transpose(1, 0, 3, 2, 4).reshape(B, S, H, dv)


def memory_attention(q, mk, mv):
    s = jnp.einsum("bshd,bmhd->bhsm", q, mk).astype(jnp.float32) * (MEM_HEAD_DIM ** -0.5)
    p = jax.nn.softmax(s, axis=-1).astype(mv.dtype)
    return jnp.einsum("bhsm,bmhd->bshd", p, mv)


def hybrid_layer(x, mem, pre_g, w_in, conv_w, a_log, dt_bias, sinks, dn_norm_g,
                 mem_norm_g, w_mem_kv, w_out, post_g):
    B, S, _ = x.shape
    h = rms_norm(x, pre_g)
    proj = h @ w_in
    sq, sk, sv, dq, dk, dv, dbeta, dalpha, mq, gate = jnp.split(
        proj, _split_points(IN_SIZES), axis=-1)

    a_out = swa_sink_attention(
        sq.reshape(B, S, SWA_HEADS, SWA_HEAD_DIM),
        sk.reshape(B, S, SWA_KV_HEADS, SWA_HEAD_DIM),
        sv.reshape(B, S, SWA_KV_HEADS, SWA_HEAD_DIM), sinks)

    qkv = jax.nn.silu(causal_depthwise_conv(jnp.concatenate([dq, dk, dv], axis=-1), conv_w))
    cq, ck, cv = jnp.split(qkv, [DN_WIDTH, 2 * DN_WIDTH], axis=-1)
    qf = l2_norm(cq.reshape(B, S, DN_HEADS, DN_HEAD_DIM))
    kf = l2_norm(ck.reshape(B, S, DN_HEADS, DN_HEAD_DIM))
    vf = cv.reshape(B, S, DN_HEADS, DN_HEAD_DIM).astype(jnp.float32)
    beta = jax.nn.sigmoid(dbeta.astype(jnp.float32))
    g = -jnp.exp(a_log.astype(jnp.float32)) * jax.nn.softplus(
        dalpha.astype(jnp.float32) + dt_bias.astype(jnp.float32))
    d_out = gated_delta_rule(qf, kf, vf, g, beta)
    d_out = rms_norm(d_out, dn_norm_g).reshape(B, S, DN_WIDTH).astype(x.dtype)

    mkv = rms_norm(mem, mem_norm_g) @ w_mem_kv
    mk, mv = jnp.split(mkv, [MEM_WIDTH], axis=-1)
    Mn = mem.shape[1]
    m_out = memory_attention(
        mq.reshape(B, S, MEM_HEADS, MEM_HEAD_DIM),
        mk.reshape(B, Mn, MEM_HEADS, MEM_HEAD_DIM),
        mv.reshape(B, Mn, MEM_HEADS, MEM_HEAD_DIM)).reshape(B, S, MEM_WIDTH)

    mixed = jnp.concatenate([a_out, d_out, m_out], axis=-1) * jax.nn.silu(gate)
    y = mixed @ w_out
    return x + rms_norm(y, post_g)


def setup_inputs(seed: int = 0) -> dict:
    key = jax.random.key(seed)
    ks = jax.random.split(key, 16)
    f32 = jnp.float32
    x = jax.random.normal(ks[0], (BATCH, SEQ, D_MODEL), f32)
    mem = jax.random.normal(ks[1], (BATCH, N_MEM, D_MODEL), f32)
    pre_norm_g = 1.0 + 0.05 * jax.random.normal(ks[2], (DEPTH, D_MODEL), f32)
    w_in = jax.random.normal(ks[3], (DEPTH, D_MODEL, IN_WIDTH), f32) * D_MODEL ** -0.5
    conv_w = jax.random.normal(ks[4], (DEPTH, DN_CONV, 3 * DN_WIDTH), f32) * DN_CONV ** -0.5
    a_log = jnp.log(jax.random.uniform(ks[5], (DEPTH, DN_HEADS), f32, 1.0, 16.0))
    dt = jnp.exp(jax.random.uniform(ks[6], (DEPTH, DN_HEADS), f32,
                                    float(np.log(1e-3)), float(np.log(1e-1))))
    dt_bias = dt + jnp.log(-jnp.expm1(-dt))
    sinks = 0.5 * jax.random.normal(ks[7], (DEPTH, SWA_HEADS), f32)
    dn_norm_g = 1.0 + 0.05 * jax.random.normal(ks[8], (DEPTH, DN_HEAD_DIM), f32)
    mem_norm_g = 1.0 + 0.05 * jax.random.normal(ks[9], (DEPTH, D_MODEL), f32)
    w_mem_kv = jax.random.normal(ks[10], (DEPTH, D_MODEL, 2 * MEM_WIDTH), f32) * D_MODEL ** -0.5
    w_out = jax.random.normal(ks[11], (DEPTH, MIX_WIDTH, D_MODEL), f32) * MIX_WIDTH ** -0.5
    post_norm_g = 1.0 + 0.05 * jax.random.normal(ks[12], (DEPTH, D_MODEL), f32)
    return {"x": x, "mem": mem, "pre_norm_g": pre_norm_g, "w_in": w_in, "conv_w": conv_w,
            "a_log": a_log, "dt_bias": dt_bias, "sinks": sinks, "dn_norm_g": dn_norm_g,
            "mem_norm_g": mem_norm_g, "w_mem_kv": w_mem_kv, "w_out": w_out,
            "post_norm_g": post_norm_g}


def reference(x, mem, pre_norm_g, w_in, conv_w, a_log, dt_bias, sinks, dn_norm_g,
              mem_norm_g, w_mem_kv, w_out, post_norm_g):
    for l in range(DEPTH):
        x = hybrid_layer(x, mem, pre_norm_g[l], w_in[l], conv_w[l], a_log[l], dt_bias[l],
                         sinks[l], dn_norm_g[l], mem_norm_g[l], w_mem_kv[l], w_out[l],
                         post_norm_g[l])
    return x
```

```python
import functools

import jax
import jax.numpy as jnp
from jax import lax
from jax.experimental import pallas as pl
from jax.experimental.pallas import tpu as pltpu

F32 = jnp.float32
BF16 = jnp.bfloat16

D_MODEL = 1024
HEAD_DIM = 64
SWA_HEADS = 8
SWA_KV_HEADS = 2
SWA_WIDTH = SWA_HEADS * HEAD_DIM
WINDOW = 128
DN_HEADS = 4
DN_WIDTH = DN_HEADS * HEAD_DIM
DN_CONV = 4
DN_CHUNK = 64
N_MEM = 256
MEM_HEADS = 4
MEM_WIDTH = MEM_HEADS * HEAD_DIM
MIX_WIDTH = 1024
EPS = 1e-6
SCALE = HEAD_DIM ** -0.5

LANES = 128
CONV_HIST = 8

C_SWA = 0
C_DN = 768
C_MQ = 1536
C_GATE = 1792
C_BA = 2816
IN_COLS = 2944

GW = 128
NG = DN_WIDTH // GW
HPG = GW // HEAD_DIM

SEQ_TILE = 512
VMEM_LIMIT_BYTES = 56 * 1024 * 1024


def _dot(a, b):
    return jnp.dot(a.astype(BF16), b.astype(BF16), preferred_element_type=F32)


def _dot_nt(a, b):
    return lax.dot_general(a.astype(BF16), b.astype(BF16), (((1,), (1,)), ((), ())),
                           preferred_element_type=F32)


def _dot_tn(a, b):
    return lax.dot_general(a.astype(BF16), b.astype(BF16), (((0,), (0,)), ((), ())),
                           preferred_element_type=F32)


def _split3(x):
    x0 = x.astype(BF16)
    r1 = x - x0.astype(F32)
    x1 = r1.astype(BF16)
    x2 = (r1 - x1.astype(F32)).astype(BF16)
    return x0, x1, x2


def _dot_sel(x, m):
    x0, x1, x2 = _split3(x)
    return (jnp.dot(x0, m, preferred_element_type=F32)
            + jnp.dot(x1, m, preferred_element_type=F32)
            + jnp.dot(x2, m, preferred_element_type=F32))


def _sel_dot(m, x):
    x0, x1, x2 = _split3(x)
    return (jnp.dot(m, x0, preferred_element_type=F32)
            + jnp.dot(m, x1, preferred_element_type=F32)
            + jnp.dot(m, x2, preferred_element_type=F32))


def _iota(shape, dim):
    return lax.broadcasted_iota(jnp.int32, shape, dim)


def _rms(x, g):
    return x * lax.rsqrt(jnp.mean(x * x, axis=-1, keepdims=True) + EPS) * g


def _silu(x):
    return x * (1.0 / (1.0 + jnp.exp(-x)))


def _block_diag(x, n):
    rows = x.shape[0]
    t = jnp.concatenate([x] * n, axis=0)
    keep = (_iota(t.shape, 0) // rows) == (_iota(t.shape, 1) // HEAD_DIM)
    return jnp.where(keep, t, 0.0)


def _layer_kernel(sinks_ref, x_ref, mem_ref, pre_g_ref, w_in_ref, conv_w_ref, alog_ref, dtb_ref,
                  dn_g_ref, mem_g_ref, w_mem_ref, w_out_ref, post_g_ref, out_ref,
                  h_ref, sq_ref, kk_ref, vv_ref, dnp_ref, qn_ref, kn_ref, vn_ref, gb_ref, bb_ref,
                  s_ref, od_ref, mkv_ref, mix_ref, *, ts):
    i = pl.program_id(1)
    nblk = ts // WINDOW
    nchunk = ts // DN_CHUNK

    lane = _iota((1, LANES), 1)
    lo = lane < HEAD_DIM

    @pl.when(i == 0)
    def _():
        s_ref[...] = jnp.zeros_like(s_ref)
        kk_ref[:, 0:WINDOW, :] = jnp.zeros((4, WINDOW, LANES), BF16)
        vv_ref[:, 0:WINDOW, :] = jnp.zeros((4, WINDOW, LANES), BF16)
        dnp_ref[0:CONV_HIST, :] = jnp.zeros((CONV_HIST, 3 * DN_WIDTH), F32)
        hm = _rms(mem_ref[0], mem_g_ref[...])
        mkv = _dot(hm, w_mem_ref[...])
        for kind in range(2):
            for p in range(MEM_HEADS // 2):
                blk = mkv[:, kind * MEM_WIDTH + p * LANES: kind * MEM_WIDTH + (p + 1) * LANES]
                mkv_ref[kind * 4 + p * 2 + 0] = jnp.where(lo, blk, 0.0).astype(BF16)
                mkv_ref[kind * 4 + p * 2 + 1] = jnp.where(lo, 0.0, blk).astype(BF16)

    h_ref[...] = _rms(x_ref[0], pre_g_ref[...]).astype(BF16)

    p_swa = jnp.dot(h_ref[...], w_in_ref[:, C_SWA:C_SWA + SWA_WIDTH + 2 * LANES],
                    preferred_element_type=F32)
    sq_ref[...] = (p_swa[:, 0:SWA_WIDTH] * SCALE).astype(BF16)
    kcur = p_swa[:, SWA_WIDTH:SWA_WIDTH + LANES]
    vcur = p_swa[:, SWA_WIDTH + LANES:SWA_WIDTH + 2 * LANES]
    for ref, cur in ((kk_ref, kcur), (vv_ref, vcur)):
        rolled = pltpu.roll(cur, HEAD_DIM, 1)
        ref[0, WINDOW:WINDOW + ts, :] = jnp.where(lo, cur, 0.0).astype(BF16)
        ref[1, WINDOW:WINDOW + ts, :] = jnp.where(lo, 0.0, rolled).astype(BF16)
        ref[2, WINDOW:WINDOW + ts, :] = jnp.where(lo, rolled, 0.0).astype(BF16)
        ref[3, WINDOW:WINDOW + ts, :] = jnp.where(lo, 0.0, cur).astype(BF16)

    qi = _iota((WINDOW, 2 * WINDOW), 0)
    kj = _iota((WINDOW, 2 * WINDOW), 1)
    band = (kj > qi) & (kj <= qi + WINDOW)

    def swa_block(j, carry):
        r0 = pl.multiple_of(j * WINDOW, WINDOW)
        first = jnp.logical_and(i == 0, j == 0)
        valid = band & ((kj >= WINDOW) | jnp.logical_not(first))
        for p in range(SWA_HEADS // 2):
            kvh = (2 * p) // (SWA_HEADS // SWA_KV_HEADS)
            q2 = sq_ref[pl.ds(r0, WINDOW), p * LANES:(p + 1) * LANES]
            acc = jnp.zeros((WINDOW, LANES), F32)
            for half in range(2):
                kv = kk_ref[2 * kvh + half, pl.ds(r0, 2 * WINDOW), :]
                vv = vv_ref[2 * kvh + half, pl.ds(r0, 2 * WINDOW), :]
                s = _dot_nt(q2, kv)
                s = jnp.where(valid, s, -jnp.inf)
                sink = sinks_ref[2 * p + half]
                m = jnp.maximum(jnp.max(s, axis=-1, keepdims=True), sink)
                e = jnp.exp(s - m)
                denom = jnp.sum(e, axis=-1, keepdims=True) + jnp.exp(sink - m)
                acc = acc + _dot(e, vv) * (1.0 / denom)
            mix_ref[pl.ds(r0, WINDOW), p * LANES:(p + 1) * LANES] = acc
        return carry

    lax.fori_loop(0, nblk, swa_block, 0)
    for ref in (kk_ref, vv_ref):
        ref[:, 0:WINDOW, :] = ref[:, ts:ts + WINDOW, :]

    dnp_ref[CONV_HIST:CONV_HIST + ts, :] = jnp.dot(
        h_ref[...], w_in_ref[:, C_DN:C_DN + 3 * DN_WIDTH], preferred_element_type=F32)
    conv = jnp.zeros((ts, 3 * DN_WIDTH), F32)
    for j in range(DN_CONV):
        off = CONV_HIST - (DN_CONV - 1) + j
        conv = conv + dnp_ref[off:off + ts, :] * conv_w_ref[j:j + 1, :]
    dnp_ref[0:CONV_HIST, :] = dnp_ref[ts:ts + CONV_HIST, :]
    conv = _silu(conv)
    head_ones = ((_iota((DN_WIDTH, DN_WIDTH), 0) // HEAD_DIM)
                 == (_iota((DN_WIDTH, DN_WIDTH), 1) // HEAD_DIM)).astype(BF16)
    cq = conv[:, 0:DN_WIDTH]
    ck = conv[:, DN_WIDTH:2 * DN_WIDTH]
    qn_ref[...] = cq * lax.rsqrt(_dot(cq * cq, head_ones) + EPS) * SCALE
    kn_ref[...] = ck * lax.rsqrt(_dot(ck * ck, head_ones) + EPS)
    vn_ref[...] = conv[:, 2 * DN_WIDTH:3 * DN_WIDTH]

    ba = jnp.dot(h_ref[...], w_in_ref[:, C_BA:C_BA + LANES], preferred_element_type=F32)
    beta = 1.0 / (1.0 + jnp.exp(-ba))
    z = ba + dtb_ref[...]
    softplus = jnp.maximum(z, 0.0) + jnp.log(1.0 + jnp.exp(-jnp.abs(z)))
    glog = -jnp.exp(alog_ref[...]) * softplus
    sel_r = _iota((LANES, GW), 0)
    sel_c = _iota((LANES, GW), 1) // HEAD_DIM
    for gidx in range(NG):
        bsel = (sel_r == gidx * HPG + sel_c).astype(BF16)
        gsel = (sel_r == DN_HEADS + gidx * HPG + sel_c).astype(BF16)
        bb_ref[gidx] = _dot_sel(beta, bsel)
        gb_ref[gidx] = _dot_sel(glog, gsel)

    cc = _iota((DN_CHUNK, GW), 0)
    cs = _iota((DN_CHUNK, GW), 1) % HEAD_DIM
    causal = cc >= cs
    strict = cc > cs
    diag = cc == cs
    eye2 = diag.astype(F32)
    tri = (_iota((DN_CHUNK, DN_CHUNK), 0) >= _iota((DN_CHUNK, DN_CHUNK), 1)).astype(BF16)
    bmask = ((_iota((GW, GW), 0) // HEAD_DIM) == (_iota((GW, GW), 1) // HEAD_DIM))

    def gdn_chunk(c, carry):
        r0 = pl.multiple_of(c * DN_CHUNK, DN_CHUNK)
        for gidx in range(NG):
            lanes = slice(gidx * GW, (gidx + 1) * GW)
            q2 = qn_ref[pl.ds(r0, DN_CHUNK), lanes]
            k2 = kn_ref[pl.ds(r0, DN_CHUNK), lanes]
            v2 = vn_ref[pl.ds(r0, DN_CHUNK), lanes]
            beta2 = bb_ref[gidx, pl.ds(r0, DN_CHUNK), :]
            gc = _sel_dot(tri, gb_ref[gidx, pl.ds(r0, DN_CHUNK), :])
            grow = jnp.sum(jnp.where(diag, gc, 0.0), axis=0, keepdims=True)
            glast = gc[DN_CHUNK - 1:DN_CHUNK, :]
            dec = jnp.exp(jnp.where(causal, gc - grow, -jnp.inf))
            eg = jnp.exp(gc)
            kb2 = k2 * beta2
            vb2 = v2 * beta2
            kq = _dot_nt(jnp.concatenate([kb2, q2], axis=0), _block_diag(k2, HPG))
            lmat = jnp.where(strict, kq[0:DN_CHUNK] * dec, 0.0)
            amat = kq[DN_CHUNK:2 * DN_CHUNK] * dec
            xinv = eye2 - lmat
            pw = lmat
            for _ in range(5):
                pw = _dot(pw, _block_diag(pw, HPG))
                xinv = xinv + _dot(xinv, _block_diag(pw, HPG))
            rhs = jnp.concatenate([_block_diag(vb2, HPG), _block_diag(kb2 * eg, HPG)], axis=1)
            uw = _dot(xinv, rhs)
            u2 = uw[:, 0:GW]
            w2 = uw[:, GW:2 * GW]
            state = s_ref[gidx]
            wq = _dot(jnp.concatenate([w2, q2 * eg], axis=0), state)
            vnew = u2 - wq[0:DN_CHUNK]
            od_ref[pl.ds(r0, DN_CHUNK), lanes] = (wq[DN_CHUNK:2 * DN_CHUNK]
                                                  + _dot(amat, _block_diag(vnew, HPG)))
            kd2 = k2 * jnp.exp(glast - gc)
            s_ref[gidx] = state * jnp.exp(glast) + jnp.where(bmask, _dot_tn(kd2, vnew), 0.0)
        return carry

    lax.fori_loop(0, nchunk, gdn_chunk, 0)
    od = od_ref[...]
    mix_ref[:, SWA_WIDTH:SWA_WIDTH + DN_WIDTH] = (
        od * lax.rsqrt(_dot(od * od, head_ones) * (1.0 / HEAD_DIM) + EPS) * dn_g_ref[...])

    mq = jnp.dot(h_ref[...], w_in_ref[:, C_MQ:C_MQ + MEM_WIDTH], preferred_element_type=F32) * SCALE
    for p in range(MEM_HEADS // 2):
        q2 = mq[:, p * LANES:(p + 1) * LANES].astype(BF16)
        acc = jnp.zeros((ts, LANES), F32)
        for half in range(2):
            s = _dot_nt(q2, mkv_ref[p * 2 + half])
            e = jnp.exp(s - jnp.max(s, axis=-1, keepdims=True))
            acc = acc + _dot(e, mkv_ref[4 + p * 2 + half]) * (1.0 / jnp.sum(e, axis=-1, keepdims=True))
        off = SWA_WIDTH + DN_WIDTH + p * LANES
        mix_ref[:, off:off + LANES] = acc

    gate = jnp.dot(h_ref[...], w_in_ref[:, C_GATE:C_GATE + MIX_WIDTH], preferred_element_type=F32)
    y = _dot(mix_ref[...] * _silu(gate), w_out_ref[...])
    out_ref[0] = x_ref[0] + _rms(y, post_g_ref[...])


def _layer(x, mem, sinks, pre_g, w_in, conv_w, alog, dtb, dn_g, mem_g, w_mem, w_out, post_g):
    b, s, d = x.shape
    ts = min(SEQ_TILE, s)
    assert s % ts == 0 and ts % WINDOW == 0 and d == D_MODEL
    const = lambda bi, si: (0, 0)
    return pl.pallas_call(
        functools.partial(_layer_kernel, ts=ts),
        out_shape=jax.ShapeDtypeStruct(x.shape, x.dtype),
        grid=(b, s // ts),
        in_specs=[
            pl.BlockSpec(memory_space=pltpu.SMEM),
            pl.BlockSpec((1, ts, d), lambda bi, si: (bi, si, 0)),
            pl.BlockSpec((1, N_MEM, d), lambda bi, si: (bi, 0, 0)),
            pl.BlockSpec((1, d), const),
            pl.BlockSpec((d, IN_COLS), const),
            pl.BlockSpec((DN_CONV, 3 * DN_WIDTH), const),
            pl.BlockSpec((1, LANES), const),
            pl.BlockSpec((1, LANES), const),
            pl.BlockSpec((1, DN_WIDTH), const),
            pl.BlockSpec((1, d), const),
            pl.BlockSpec((d, 2 * MEM_WIDTH), const),
            pl.BlockSpec((MIX_WIDTH, d), const),
            pl.BlockSpec((1, d), const),
        ],
        out_specs=pl.BlockSpec((1, ts, d), lambda bi, si: (bi, si, 0)),
        scratch_shapes=[
            pltpu.VMEM((ts, d), BF16),
            pltpu.VMEM((ts, SWA_WIDTH), BF16),
            pltpu.VMEM((4, WINDOW + ts, LANES), BF16),
            pltpu.VMEM((4, WINDOW + ts, LANES), BF16),
            pltpu.VMEM((CONV_HIST + ts, 3 * DN_WIDTH), F32),
            pltpu.VMEM((ts, DN_WIDTH), F32),
            pltpu.VMEM((ts, DN_WIDTH), F32),
            pltpu.VMEM((ts, DN_WIDTH), F32),
            pltpu.VMEM((NG, ts, GW), F32),
            pltpu.VMEM((NG, ts, GW), F32),
            pltpu.VMEM((NG, GW, GW), F32),
            pltpu.VMEM((ts, DN_WIDTH), F32),
            pltpu.VMEM((8, N_MEM, LANES), BF16),
            pltpu.VMEM((ts, MIX_WIDTH), F32),
        ],
        compiler_params=pltpu.CompilerParams(
            dimension_semantics=("arbitrary", "arbitrary"),
            vmem_limit_bytes=VMEM_LIMIT_BYTES),
        name="hybrid_layer",
    )(sinks, x, mem, pre_g, w_in, conv_w, alog, dtb, dn_g, mem_g, w_mem, w_out, post_g)


def _reorder_w_in(w):
    o = 0
    parts = {}
    for name, width in (("sq", 512), ("sk", 128), ("sv", 128), ("dq", 256), ("dk", 256), ("dv", 256),
                        ("beta", 4), ("alpha", 4), ("mq", 256), ("gate", 1024)):
        parts[name] = w[:, o:o + width]
        o += width
    pad = jnp.zeros((w.shape[0], LANES - 2 * DN_HEADS), w.dtype)
    return jnp.concatenate([parts[n] for n in ("sq", "sk", "sv", "dq", "dk", "dv", "mq", "gate",
                                               "beta", "alpha")] + [pad], axis=1).astype(BF16)


def _lane_place(v, start):
    return jnp.zeros((1, LANES), F32).at[0, start:start + v.shape[0]].set(v.astype(F32))


@jax.jit
def kernel(x, mem, pre_norm_g, w_in, conv_w, a_log, dt_bias, sinks, dn_norm_g, mem_norm_g, w_mem_kv,
           w_out, post_norm_g):
    depth = w_in.shape[0]
    for l in range(depth):
        x = _layer(
            x, mem, sinks[l].astype(F32),
            pre_norm_g[l][None, :], _reorder_w_in(w_in[l]), conv_w[l],
            _lane_place(a_log[l], DN_HEADS), _lane_place(dt_bias[l], DN_HEADS),
            jnp.tile(dn_norm_g[l], DN_HEADS)[None, :], mem_norm_g[l][None, :],
            w_mem_kv[l].astype(BF16), w_out[l].astype(BF16), post_norm_g[l][None, :])
    return x
```

```python
import functools

import jax
import jax.numpy as jnp
from jax import lax
from jax.experimental import pallas as pl
from jax.experimental.pallas import tpu as pltpu

F32 = jnp.float32
BF16 = jnp.bfloat16

D_MODEL = 1024
HEAD_DIM = 64
SWA_HEADS = 8
SWA_KV_HEADS = 2
SWA_WIDTH = SWA_HEADS * HEAD_DIM
WINDOW = 128
DN_HEADS = 4
DN_WIDTH = DN_HEADS * HEAD_DIM
DN_CONV = 4
DN_CHUNK = 64
N_MEM = 256
MEM_HEADS = 4
MEM_WIDTH = MEM_HEADS * HEAD_DIM
MIX_WIDTH = 1024
EPS = 1e-6
SCALE = HEAD_DIM ** -0.5

LANES = 128
CONV_HIST = 8

C_SWA = 0
C_DN = 768
C_MQ = 1536
C_GATE = 1792
C_BA = 2816
IN_COLS = 2944

GW = 128
NG = DN_WIDTH // GW
HPG = GW // HEAD_DIM

SEQ_TILE = 512
GDN_BATCH = 8
VMEM_LIMIT_BYTES = 56 * 1024 * 1024


def _dot(a, b):
    return jnp.dot(a.astype(BF16), b.astype(BF16), preferred_element_type=F32)


def _dot_nt(a, b):
    return lax.dot_general(a.astype(BF16), b.astype(BF16), (((1,), (1,)), ((), ())),
                           preferred_element_type=F32)


def _dot_tn(a, b):
    return lax.dot_general(a.astype(BF16), b.astype(BF16), (((0,), (0,)), ((), ())),
                           preferred_element_type=F32)


def _split3(x):
    x0 = x.astype(BF16)
    r1 = x - x0.astype(F32)
    x1 = r1.astype(BF16)
    x2 = (r1 - x1.astype(F32)).astype(BF16)
    return x0, x1, x2


def _dot_sel(x, m):
    x0, x1, x2 = _split3(x)
    return (jnp.dot(x0, m, preferred_element_type=F32)
            + jnp.dot(x1, m, preferred_element_type=F32)
            + jnp.dot(x2, m, preferred_element_type=F32))


def _sel_dot(m, x):
    x0, x1, x2 = _split3(x)
    return (jnp.dot(m, x0, preferred_element_type=F32)
            + jnp.dot(m, x1, preferred_element_type=F32)
            + jnp.dot(m, x2, preferred_element_type=F32))


def _iota(shape, dim):
    return lax.broadcasted_iota(jnp.int32, shape, dim)


def _rms(x, g):
    return x * lax.rsqrt(jnp.mean(x * x, axis=-1, keepdims=True) + EPS) * g


def _silu(x):
    return x * (1.0 / (1.0 + jnp.exp(-x)))


def _block_diag(x, n):
    rows = x.shape[0]
    t = jnp.concatenate([x] * n, axis=0)
    keep = (_iota(t.shape, 0) // rows) == (_iota(t.shape, 1) // HEAD_DIM)
    return jnp.where(keep, t, 0.0)


def _layer_kernel(sinks_ref, x_ref, mem_ref, pre_g_ref, w_in_ref, conv_w_ref, alog_ref, dtb_ref,
                  dn_g_ref, mem_g_ref, w_mem_ref, w_out_ref, post_g_ref, out_ref,
                  h_ref, sq_ref, kk_ref, vv_ref, dnp_ref, qn_ref, kn_ref, vn_ref, gb_ref, bb_ref,
                  s_ref, od_ref, mkv_ref, mix_ref, *, ts):
    i = pl.program_id(1)
    nblk = ts // WINDOW
    nchunk = ts // DN_CHUNK

    lane = _iota((1, LANES), 1)
    lo = lane < HEAD_DIM

    @pl.when(i == 0)
    def _():
        s_ref[...] = jnp.zeros_like(s_ref)
        kk_ref[:, 0:WINDOW, :] = jnp.zeros((4, WINDOW, LANES), BF16)
        vv_ref[:, 0:WINDOW, :] = jnp.zeros((4, WINDOW, LANES), BF16)
        dnp_ref[0:CONV_HIST, :] = jnp.zeros((CONV_HIST, 3 * DN_WIDTH), F32)
        hm = _rms(mem_ref[0], mem_g_ref[...])
        mkv = _dot(hm, w_mem_ref[...])
        for kind in range(2):
            for p in range(MEM_HEADS // 2):
                blk = mkv[:, kind * MEM_WIDTH + p * LANES: kind * MEM_WIDTH + (p + 1) * LANES]
                mkv_ref[kind * 4 + p * 2 + 0] = jnp.where(lo, blk, 0.0).astype(BF16)
                mkv_ref[kind * 4 + p * 2 + 1] = jnp.where(lo, 0.0, blk).astype(BF16)

    h_ref[...] = _rms(x_ref[0], pre_g_ref[...]).astype(BF16)

    p_swa = jnp.dot(h_ref[...], w_in_ref[:, C_SWA:C_SWA + SWA_WIDTH + 2 * LANES],
                    preferred_element_type=F32)
    sq_ref[...] = (p_swa[:, 0:SWA_WIDTH] * SCALE).astype(BF16)
    kcur = p_swa[:, SWA_WIDTH:SWA_WIDTH + LANES]
    vcur = p_swa[:, SWA_WIDTH + LANES:SWA_WIDTH + 2 * LANES]
    for ref, cur in ((kk_ref, kcur), (vv_ref, vcur)):
        rolled = pltpu.roll(cur, HEAD_DIM, 1)
        ref[0, WINDOW:WINDOW + ts, :] = jnp.where(lo, cur, 0.0).astype(BF16)
        ref[1, WINDOW:WINDOW + ts, :] = jnp.where(lo, 0.0, rolled).astype(BF16)
        ref[2, WINDOW:WINDOW + ts, :] = jnp.where(lo, rolled, 0.0).astype(BF16)
        ref[3, WINDOW:WINDOW + ts, :] = jnp.where(lo, 0.0, cur).astype(BF16)

    qi = _iota((WINDOW, 2 * WINDOW), 0)
    kj = _iota((WINDOW, 2 * WINDOW), 1)
    band = (kj > qi) & (kj <= qi + WINDOW)

    def swa_block(j, carry):
        r0 = pl.multiple_of(j * WINDOW, WINDOW)
        first = jnp.logical_and(i == 0, j == 0)
        valid = band & ((kj >= WINDOW) | jnp.logical_not(first))
        for p in range(SWA_HEADS // 2):
            kvh = (2 * p) // (SWA_HEADS // SWA_KV_HEADS)
            q2 = sq_ref[pl.ds(r0, WINDOW), p * LANES:(p + 1) * LANES]
            acc = jnp.zeros((WINDOW, LANES), F32)
            for half in range(2):
                kv = kk_ref[2 * kvh + half, pl.ds(r0, 2 * WINDOW), :]
                vv = vv_ref[2 * kvh + half, pl.ds(r0, 2 * WINDOW), :]
                s = _dot_nt(q2, kv)
                s = jnp.where(valid, s, -jnp.inf)
                sink = sinks_ref[2 * p + half]
                m = jnp.maximum(jnp.max(s, axis=-1, keepdims=True), sink)
                e = jnp.exp(s - m)
                denom = jnp.sum(e, axis=-1, keepdims=True) + jnp.exp(sink - m)
                acc = acc + _dot(e, vv) * (1.0 / denom)
            mix_ref[pl.ds(r0, WINDOW), p * LANES:(p + 1) * LANES] = acc
        return carry

    lax.fori_loop(0, nblk, swa_block, 0)
    for ref in (kk_ref, vv_ref):
        ref[:, 0:WINDOW, :] = ref[:, ts:ts + WINDOW, :]

    dnp_ref[CONV_HIST:CONV_HIST + ts, :] = jnp.dot(
        h_ref[...], w_in_ref[:, C_DN:C_DN + 3 * DN_WIDTH], preferred_element_type=F32)
    conv = jnp.zeros((ts, 3 * DN_WIDTH), F32)
    for j in range(DN_CONV):
        off = CONV_HIST - (DN_CONV - 1) + j
        conv = conv + dnp_ref[off:off + ts, :] * conv_w_ref[j:j + 1, :]
    dnp_ref[0:CONV_HIST, :] = dnp_ref[ts:ts + CONV_HIST, :]
    conv = _silu(conv)
    head_ones = ((_iota((DN_WIDTH, DN_WIDTH), 0) // HEAD_DIM)
                 == (_iota((DN_WIDTH, DN_WIDTH), 1) // HEAD_DIM)).astype(BF16)
    cq = conv[:, 0:DN_WIDTH]
    ck = conv[:, DN_WIDTH:2 * DN_WIDTH]
    qn_ref[...] = cq * lax.rsqrt(_dot(cq * cq, head_ones) + EPS) * SCALE
    kn_ref[...] = ck * lax.rsqrt(_dot(ck * ck, head_ones) + EPS)
    vn_ref[...] = conv[:, 2 * DN_WIDTH:3 * DN_WIDTH]

    ba = jnp.dot(h_ref[...], w_in_ref[:, C_BA:C_BA + LANES], preferred_element_type=F32)
    beta = 1.0 / (1.0 + jnp.exp(-ba))
    z = ba + dtb_ref[...]
    softplus = jnp.maximum(z, 0.0) + jnp.log(1.0 + jnp.exp(-jnp.abs(z)))
    glog = -jnp.exp(alog_ref[...]) * softplus
    sel_r = _iota((LANES, GW), 0)
    sel_c = _iota((LANES, GW), 1) // HEAD_DIM
    for gidx in range(NG):
        bsel = (sel_r == gidx * HPG + sel_c).astype(BF16)
        gsel = (sel_r == DN_HEADS + gidx * HPG + sel_c).astype(BF16)
        bb_ref[gidx] = _dot_sel(beta, bsel)
        gb_ref[gidx] = _dot_sel(glog, gsel)

    cc = _iota((DN_CHUNK, GW), 0)
    cs = _iota((DN_CHUNK, GW), 1) % HEAD_DIM
    causal = cc >= cs
    strict = cc > cs
    diag = cc == cs
    eye2 = diag.astype(F32)
    tri = (_iota((DN_CHUNK, DN_CHUNK), 0) >= _iota((DN_CHUNK, DN_CHUNK), 1)).astype(BF16)
    bmask = ((_iota((GW, GW), 0) // HEAD_DIM) == (_iota((GW, GW), 1) // HEAD_DIM))

    eye_g = _iota((GW, GW), 0) == _iota((GW, GW), 1)

    items = [(slice(c * DN_CHUNK, (c + 1) * DN_CHUNK), gidx, slice(gidx * GW, (gidx + 1) * GW))
             for c in range(nchunk) for gidx in range(NG)]
    local = []
    for b0 in range(0, len(items), GDN_BATCH):
        batch = items[b0:b0 + GDN_BATCH]
        n = len(batch)
        q2 = [qn_ref[rows, lanes] for rows, _, lanes in batch]
        k2 = [kn_ref[rows, lanes] for rows, _, lanes in batch]
        gc = [_sel_dot(tri, gb_ref[gidx, rows, :]) for rows, gidx, _ in batch]
        kb2 = [k2[t] * bb_ref[batch[t][1], batch[t][0], :] for t in range(n)]
        kq = [_dot_nt(jnp.concatenate([kb2[t], q2[t]], axis=0), _block_diag(k2[t], HPG)) for t in range(n)]
        grow = [jnp.sum(jnp.where(diag, g, 0.0), axis=0, keepdims=True) for g in gc]
        dec = [jnp.exp(jnp.where(causal, gc[t] - grow[t], -jnp.inf)) for t in range(n)]
        nmat = [jnp.where(strict, -(kq[t][0:DN_CHUNK] * dec[t]), 0.0) for t in range(n)]
        amat = [kq[t][DN_CHUNK:2 * DN_CHUNK] * dec[t] for t in range(n)]
        xinv = [eye2 + m for m in nmat]
        pw = [_dot(m, _block_diag(m, HPG)) for m in nmat]
        for r in range(1, 6):
            if r < 5:
                res = [_dot(jnp.concatenate([pw[t], xinv[t]], axis=0), _block_diag(pw[t], HPG)) for t in range(n)]
                pw = [x[0:DN_CHUNK] for x in res]
                xinv = [xinv[t] + res[t][DN_CHUNK:2 * DN_CHUNK] for t in range(n)]
            else:
                xinv = [xinv[t] + _dot(xinv[t], _block_diag(pw[t], HPG)) for t in range(n)]
        eg = [jnp.exp(g) for g in gc]
        rhs = []
        for t in range(n):
            rows, gidx, lanes = batch[t]
            vb2 = vn_ref[rows, lanes] * bb_ref[gidx, rows, :]
            rhs.append(jnp.concatenate([_block_diag(vb2, HPG), _block_diag(kb2[t] * eg[t], HPG)], axis=1))
        uw = [_dot(xinv[t], rhs[t]) for t in range(n)]
        glast = [g[DN_CHUNK - 1:DN_CHUNK, :] for g in gc]
        kuw = [_dot_tn(k2[t] * jnp.exp(glast[t] - gc[t]), uw[t]) for t in range(n)]
        auw = [_dot(amat[t], jnp.concatenate([_block_diag(uw[t][:, 0:GW], HPG),
                                              _block_diag(uw[t][:, GW:2 * GW], HPG)], axis=1))
               for t in range(n)]
        for t in range(n):
            rows, gidx, lanes = batch[t]
            mmat = jnp.where(bmask, jnp.where(eye_g, jnp.exp(glast[t]), 0.0) - kuw[t][:, GW:2 * GW], 0.0)
            rmat = jnp.where(bmask, kuw[t][:, 0:GW], 0.0)
            qeff = q2[t] * eg[t] - auw[t][:, GW:2 * GW]
            local.append((rows, gidx, lanes, mmat, rmat, qeff, auw[t][:, 0:GW]))

    states = [s_ref[gidx] for gidx in range(NG)]
    for rows, gidx, lanes, mmat, rmat, qeff, oloc in local:
        sb = states[gidx].astype(BF16)
        od_ref[rows, lanes] = oloc + _dot(qeff, sb)
        states[gidx] = _dot(mmat, sb) + rmat
    for gidx in range(NG):
        s_ref[gidx] = states[gidx]
    od = od_ref[...]
    mix_ref[:, SWA_WIDTH:SWA_WIDTH + DN_WIDTH] = (
        od * lax.rsqrt(_dot(od * od, head_ones) * (1.0 / HEAD_DIM) + EPS) * dn_g_ref[...])

    mq = jnp.dot(h_ref[...], w_in_ref[:, C_MQ:C_MQ + MEM_WIDTH], preferred_element_type=F32) * SCALE
    for p in range(MEM_HEADS // 2):
        q2 = mq[:, p * LANES:(p + 1) * LANES].astype(BF16)
        acc = jnp.zeros((ts, LANES), F32)
        for half in range(2):
            s = _dot_nt(q2, mkv_ref[p * 2 + half])
            e = jnp.exp(s - jnp.max(s, axis=-1, keepdims=True))
            acc = acc + _dot(e, mkv_ref[4 + p * 2 + half]) * (1.0 / jnp.sum(e, axis=-1, keepdims=True))
        off = SWA_WIDTH + DN_WIDTH + p * LANES
        mix_ref[:, off:off + LANES] = acc

    gate = jnp.dot(h_ref[...], w_in_ref[:, C_GATE:C_GATE + MIX_WIDTH], preferred_element_type=F32)
    y = _dot(mix_ref[...] * _silu(gate), w_out_ref[...])
    out_ref[0] = x_ref[0] + _rms(y, post_g_ref[...])


def _layer(x, mem, sinks, pre_g, w_in, conv_w, alog, dtb, dn_g, mem_g, w_mem, w_out, post_g):
    b, s, d = x.shape
    ts = min(SEQ_TILE, s)
    assert s % ts == 0 and ts % WINDOW == 0 and d == D_MODEL
    const = lambda bi, si: (0, 0)
    return pl.pallas_call(
        functools.partial(_layer_kernel, ts=ts),
        out_shape=jax.ShapeDtypeStruct(x.shape, x.dtype),
        grid=(b, s // ts),
        in_specs=[
            pl.BlockSpec(memory_space=pltpu.SMEM),
            pl.BlockSpec((1, ts, d), lambda bi, si: (bi, si, 0)),
            pl.BlockSpec((1, N_MEM, d), lambda bi, si: (bi, 0, 0)),
            pl.BlockSpec((1, d), const),
            pl.BlockSpec((d, IN_COLS), const),
            pl.BlockSpec((DN_CONV, 3 * DN_WIDTH), const),
            pl.BlockSpec((1, LANES), const),
            pl.BlockSpec((1, LANES), const),
            pl.BlockSpec((1, DN_WIDTH), const),
            pl.BlockSpec((1, d), const),
            pl.BlockSpec((d, 2 * MEM_WIDTH), const),
            pl.BlockSpec((MIX_WIDTH, d), const),
            pl.BlockSpec((1, d), const),
        ],
        out_specs=pl.BlockSpec((1, ts, d), lambda bi, si: (bi, si, 0)),
        scratch_shapes=[
            pltpu.VMEM((ts, d), BF16),
            pltpu.VMEM((ts, SWA_WIDTH), BF16),
            pltpu.VMEM((4, WINDOW + ts, LANES), BF16),
            pltpu.VMEM((4, WINDOW + ts, LANES), BF16),
            pltpu.VMEM((CONV_HIST + ts, 3 * DN_WIDTH), F32),
            pltpu.VMEM((ts, DN_WIDTH), F32),
            pltpu.VMEM((ts, DN_WIDTH), F32),
            pltpu.VMEM((ts, DN_WIDTH), F32),
            pltpu.VMEM((NG, ts, GW), F32),
            pltpu.VMEM((NG, ts, GW), F32),
            pltpu.VMEM((NG, GW, GW), F32),
            pltpu.VMEM((ts, DN_WIDTH), F32),
            pltpu.VMEM((8, N_MEM, LANES), BF16),
            pltpu.VMEM((ts, MIX_WIDTH), F32),
        ],
        compiler_params=pltpu.CompilerParams(
            dimension_semantics=("arbitrary", "arbitrary"),
            vmem_limit_bytes=VMEM_LIMIT_BYTES),
        name="hybrid_layer",
    )(sinks, x, mem, pre_g, w_in, conv_w, alog, dtb, dn_g, mem_g, w_mem, w_out, post_g)


def _reorder_w_in(w):
    o = 0
    parts = {}
    for name, width in (("sq", 512), ("sk", 128), ("sv", 128), ("dq", 256), ("dk", 256), ("dv", 256),
                        ("beta", 4), ("alpha", 4), ("mq", 256), ("gate", 1024)):
        parts[name] = w[:, o:o + width]
        o += width
    pad = jnp.zeros((w.shape[0], LANES - 2 * DN_HEADS), w.dtype)
    return jnp.concatenate([parts[n] for n in ("sq", "sk", "sv", "dq", "dk", "dv", "mq", "gate",
                                               "beta", "alpha")] + [pad], axis=1).astype(BF16)


def _lane_place(v, start):
    return jnp.zeros((1, LANES), F32).at[0, start:start + v.shape[0]].set(v.astype(F32))


@jax.jit
def kernel(x, mem, pre_norm_g, w_in, conv_w, a_log, dt_bias, sinks, dn_norm_g, mem_norm_g, w_mem_kv,
           w_out, post_norm_g):
    depth = w_in.shape[0]
    for l in range(depth):
        x = _layer(
            x, mem, sinks[l].astype(F32),
            pre_norm_g[l][None, :], _reorder_w_in(w_in[l]), conv_w[l],
            _lane_place(a_log[l], DN_HEADS), _lane_place(dt_bias[l], DN_HEADS),
            jnp.tile(dn_norm_g[l], DN_HEADS)[None, :], mem_norm_g[l][None, :],
            w_mem_kv[l].astype(BF16), w_out[l].astype(BF16), post_norm_g[l][None, :])
    return x
```

```python
import functools

import jax
import jax.numpy as jnp
from jax import lax
from jax.experimental import pallas as pl
from jax.experimental.pallas import tpu as pltpu

F32 = jnp.float32
BF16 = jnp.bfloat16

D_MODEL = 1024
HEAD_DIM = 64
SWA_HEADS = 8
SWA_KV_HEADS = 2
SWA_WIDTH = SWA_HEADS * HEAD_DIM
WINDOW = 128
DN_HEADS = 4
DN_WIDTH = DN_HEADS * HEAD_DIM
DN_CONV = 4
DN_CHUNK = 64
N_MEM = 256
MEM_HEADS = 4
MEM_WIDTH = MEM_HEADS * HEAD_DIM
MIX_WIDTH = 1024
EPS = 1e-6
SCALE = HEAD_DIM ** -0.5

LANES = 128
MXU_N = 256
CONV_HIST = 8

C_SWA = 0
C_DN = 768
C_MQ = 1536
C_GATE = 1792
C_BA = 2816
IN_COLS = 2944

GW = 128
NG = DN_WIDTH // GW
HPG = GW // HEAD_DIM

SEQ_TILE = 512
GDN_BATCH = 8
MEM_ROWS = 256
OUT_ROWS = 128
CONV_DELAY = 3
SWA_DELAY = 6
GDN_DELAY = 7
MEM_DELAY = 16
VMEM_LIMIT_BYTES = 56 * 1024 * 1024


def _dot(a, b):
    return jnp.dot(a.astype(BF16), b.astype(BF16), preferred_element_type=F32)


def _dot_nt(a, b):
    return lax.dot_general(a.astype(BF16), b.astype(BF16), (((1,), (1,)), ((), ())),
                           preferred_element_type=F32)


def _dot_tn(a, b):
    return lax.dot_general(a.astype(BF16), b.astype(BF16), (((0,), (0,)), ((), ())),
                           preferred_element_type=F32)


def _split3(x):
    x0 = x.astype(BF16)
    r1 = x - x0.astype(F32)
    x1 = r1.astype(BF16)
    x2 = (r1 - x1.astype(F32)).astype(BF16)
    return x0, x1, x2


def _dot_sel(x, m):
    x0, x1, x2 = _split3(x)
    return (jnp.dot(x0, m, preferred_element_type=F32)
            + jnp.dot(x1, m, preferred_element_type=F32)
            + jnp.dot(x2, m, preferred_element_type=F32))


def _sel_dot(m, x):
    x0, x1, x2 = _split3(x)
    return (jnp.dot(m, x0, preferred_element_type=F32)
            + jnp.dot(m, x1, preferred_element_type=F32)
            + jnp.dot(m, x2, preferred_element_type=F32))


def _iota(shape, dim):
    return lax.broadcasted_iota(jnp.int32, shape, dim)


def _rms(x, g):
    return x * lax.rsqrt(jnp.mean(x * x, axis=-1, keepdims=True) + EPS) * g


def _silu(x):
    return x * (1.0 / (1.0 + jnp.exp(-x)))


def _round_robin(streams):
    streams = list(streams)
    while streams:
        for s in list(streams):
            try:
                next(s)
            except StopIteration:
                streams.remove(s)


def _delayed(n, gen):
    for _ in range(n):
        yield
    yield from gen


def _layer_kernel(sinks_ref, x_ref, mem_ref, pre_g_ref, w_in_ref, conv_w_ref, alog_ref, dtb_ref,
                  dn_g_ref, mem_g_ref, w_mem_ref, w_out_ref, post_g_ref, out_ref,
                  h_ref, sq_ref, kk_ref, vv_ref, dnp_ref, qn_ref, kn_ref, vn_ref, gb_ref, bb_ref,
                  s_ref, mkv_ref, mq_ref, gate_ref, mix_ref, *, ts):
    i = pl.program_id(1)
    nblk = ts // WINDOW
    nchunk = ts // DN_CHUNK

    lane = _iota((1, LANES), 1)
    lo = lane < HEAD_DIM

    @pl.when(i == 0)
    def _():
        s_ref[...] = jnp.zeros_like(s_ref)
        kk_ref[:, 0:WINDOW, :] = jnp.zeros((4, WINDOW, LANES), BF16)
        vv_ref[:, 0:WINDOW, :] = jnp.zeros((4, WINDOW, LANES), BF16)
        dnp_ref[0:CONV_HIST, :] = jnp.zeros((CONV_HIST, 3 * DN_WIDTH), F32)
        hm = _rms(mem_ref[0], mem_g_ref[...])
        mkv = _dot(hm, w_mem_ref[...])
        for kind in range(2):
            for p in range(MEM_HEADS // 2):
                blk = mkv[:, kind * MEM_WIDTH + p * LANES: kind * MEM_WIDTH + (p + 1) * LANES]
                mkv_ref[kind * 4 + p * 2 + 0] = jnp.where(lo, blk, 0.0).astype(BF16)
                mkv_ref[kind * 4 + p * 2 + 1] = jnp.where(lo, 0.0, blk).astype(BF16)

    h_ref[...] = _rms(x_ref[0], pre_g_ref[...]).astype(BF16)

    def proj(c0, width):
        return jnp.dot(h_ref[...], w_in_ref[:, c0:c0 + width], preferred_element_type=F32)

    def proj_stream():
        for c in range(3 * DN_WIDTH // MXU_N):
            dnp_ref[CONV_HIST:CONV_HIST + ts, c * MXU_N:(c + 1) * MXU_N] = proj(C_DN + c * MXU_N, MXU_N)
            yield
        for c in range(SWA_WIDTH // MXU_N):
            sq_ref[:, c * MXU_N:(c + 1) * MXU_N] = (proj(C_SWA + c * MXU_N, MXU_N) * SCALE).astype(BF16)
            yield
        kvcur = proj(C_SWA + SWA_WIDTH, 2 * LANES)
        for ref, cur in ((kk_ref, kvcur[:, 0:LANES]), (vv_ref, kvcur[:, LANES:2 * LANES])):
            rolled = pltpu.roll(cur, HEAD_DIM, 1)
            ref[0, WINDOW:WINDOW + ts, :] = jnp.where(lo, cur, 0.0).astype(BF16)
            ref[1, WINDOW:WINDOW + ts, :] = jnp.where(lo, 0.0, rolled).astype(BF16)
            ref[2, WINDOW:WINDOW + ts, :] = jnp.where(lo, rolled, 0.0).astype(BF16)
            ref[3, WINDOW:WINDOW + ts, :] = jnp.where(lo, 0.0, cur).astype(BF16)
        yield
        ba = proj(C_BA, LANES)
        beta = 1.0 / (1.0 + jnp.exp(-ba))
        z = ba + dtb_ref[...]
        softplus = jnp.maximum(z, 0.0) + jnp.log(1.0 + jnp.exp(-jnp.abs(z)))
        glog = -jnp.exp(alog_ref[...]) * softplus
        sel_r = _iota((LANES, DN_WIDTH), 0)
        sel_c = _iota((LANES, DN_WIDTH), 1) // HEAD_DIM
        bb_ref[...] = _dot(beta, (sel_r == sel_c).astype(BF16))
        gb_ref[...] = _dot_sel(glog, (sel_r == DN_HEADS + sel_c).astype(BF16))
        yield
        mq_ref[...] = (proj(C_MQ, MEM_WIDTH) * SCALE).astype(BF16)
        yield
        for c in range(MIX_WIDTH // MXU_N):
            gate_ref[:, c * MXU_N:(c + 1) * MXU_N] = proj(C_GATE + c * MXU_N, MXU_N)
            yield

    qi = _iota((WINDOW, 2 * WINDOW), 0)
    kj = _iota((WINDOW, 2 * WINDOW), 1)
    band = (kj > qi) & (kj <= qi + WINDOW)
    band0 = band & ((kj >= WINDOW) | (i > 0))
    upper = _iota((2 * WINDOW, 1), 0) < WINDOW
    group = SWA_HEADS // SWA_KV_HEADS

    def swa_stream():
        for j in range(nblk):
            rows = slice(j * WINDOW, (j + 1) * WINDOW)
            krows = slice(j * WINDOW, (j + 2) * WINDOW)
            valid = band0 if j == 0 else band
            valid = jnp.concatenate([valid, valid], axis=0)
            for kvh in range(SWA_KV_HEADS):
                p0 = kvh * (group // 2)
                q_st = jnp.concatenate([sq_ref[rows, p0 * LANES:(p0 + 1) * LANES],
                                        sq_ref[rows, (p0 + 1) * LANES:(p0 + 2) * LANES]], axis=0)
                scores = [_dot_nt(q_st, kk_ref[2 * kvh + half, krows, :]) for half in range(2)]
                yield
                acc = None
                for half in range(2):
                    s = jnp.where(valid, scores[half], -jnp.inf)
                    sink = jnp.where(upper, sinks_ref[2 * p0 + half], sinks_ref[2 * p0 + 2 + half])
                    m = jnp.maximum(jnp.max(s, axis=-1, keepdims=True), sink)
                    e = jnp.exp(s - m)
                    denom = jnp.sum(e, axis=-1, keepdims=True) + jnp.exp(sink - m)
                    o = _dot(e, vv_ref[2 * kvh + half, krows, :]) * (1.0 / denom)
                    acc = o if acc is None else acc + o
                    yield
                mix_ref[rows, p0 * LANES:(p0 + 1) * LANES] = acc[0:WINDOW]
                mix_ref[rows, (p0 + 1) * LANES:(p0 + 2) * LANES] = acc[WINDOW:2 * WINDOW]
        for ref in (kk_ref, vv_ref):
            ref[:, 0:WINDOW, :] = ref[:, ts:ts + WINDOW, :]

    head_ones = ((_iota((DN_WIDTH, DN_WIDTH), 0) // HEAD_DIM)
                 == (_iota((DN_WIDTH, DN_WIDTH), 1) // HEAD_DIM)).astype(BF16)

    def conv_stream():
        for part, dst in enumerate((qn_ref, kn_ref, vn_ref)):
            cols = slice(part * DN_WIDTH, (part + 1) * DN_WIDTH)
            conv = jnp.zeros((ts, DN_WIDTH), F32)
            for j in range(DN_CONV):
                off = CONV_HIST - (DN_CONV - 1) + j
                conv = conv + dnp_ref[off:off + ts, cols] * conv_w_ref[j:j + 1, cols]
            conv = _silu(conv)
            if part == 0:
                dst[...] = conv * lax.rsqrt(_dot(conv * conv, head_ones) + EPS) * SCALE
            elif part == 1:
                dst[...] = conv * lax.rsqrt(_dot(conv * conv, head_ones) + EPS)
            else:
                dst[...] = conv
            yield
        dnp_ref[0:CONV_HIST, :] = dnp_ref[ts:ts + CONV_HIST, :]

    cc = _iota((DN_CHUNK, GW), 0)
    cs = _iota((DN_CHUNK, GW), 1) % HEAD_DIM
    causal = cc >= cs
    strict = cc > cs
    diag = cc == cs
    eye2 = diag.astype(F32)
    lo_g = _iota((DN_CHUNK, GW), 1) < HEAD_DIM
    tri = (_iota((DN_CHUNK, DN_CHUNK), 0) >= _iota((DN_CHUNK, DN_CHUNK), 1)).astype(BF16)
    bd_mask = ((_iota((GW, GW), 0) // HEAD_DIM) == (_iota((GW, GW), 1) // HEAD_DIM)).astype(BF16)

    def bd(x):
        xb = x.astype(BF16)
        return jnp.concatenate([xb] * HPG, axis=0) * bd_mask

    def pack_diag(x):
        return jnp.where(lo_g, x[0:HEAD_DIM], x[HEAD_DIM:2 * HEAD_DIM])

    items = [(slice(c * DN_CHUNK, (c + 1) * DN_CHUNK), slice(gidx * GW, (gidx + 1) * GW), gidx)
             for c in range(nchunk) for gidx in range(NG)]

    local = []

    def gdn_stream():
        for b0 in range(0, len(items), GDN_BATCH):
            batch = items[b0:b0 + GDN_BATCH]
            n = len(batch)
            q2 = [qn_ref[rows, lanes] for rows, lanes, _ in batch]
            k2 = [kn_ref[rows, lanes] for rows, lanes, _ in batch]
            beta2 = [bb_ref[rows, lanes] for rows, lanes, _ in batch]
            gcs = {}
            for rows, _, _ in batch:
                if rows.start not in gcs:
                    gcs[rows.start] = _sel_dot(tri, gb_ref[rows, :])
            gc = [gcs[rows.start][:, lanes] for rows, lanes, _ in batch]
            kb2 = [k2[t] * beta2[t] for t in range(n)]
            kq = [_dot_nt(jnp.concatenate([kb2[t], q2[t]], axis=0), bd(k2[t])) for t in range(n)]
            yield
            grow = [jnp.sum(jnp.where(diag, g, 0.0), axis=0, keepdims=True) for g in gc]
            dec = [jnp.exp(jnp.where(causal, gc[t] - grow[t], -jnp.inf)) for t in range(n)]
            nmat = [jnp.where(strict, -(kq[t][0:DN_CHUNK] * dec[t]), 0.0) for t in range(n)]
            amat = [kq[t][DN_CHUNK:2 * DN_CHUNK] * dec[t] for t in range(n)]
            xinv = [eye2 + m for m in nmat]
            pw = [_dot(m, bd(m)) for m in nmat]
            yield
            for r in range(1, 6):
                if r < 5:
                    res = [_dot(pw[t], jnp.concatenate([bd(pw[t]), bd(xinv[t])], axis=1)) for t in range(n)]
                    pw = [x[:, 0:GW] for x in res]
                    xinv = [xinv[t] + res[t][:, GW:2 * GW] for t in range(n)]
                else:
                    xinv = [xinv[t] + _dot(pw[t], bd(xinv[t])) for t in range(n)]
                yield
            eg = [jnp.exp(g) for g in gc]
            rhs = []
            for t in range(n):
                rows, lanes, _ = batch[t]
                rhs.append(jnp.concatenate([bd(vn_ref[rows, lanes] * beta2[t]), bd(kb2[t] * eg[t])], axis=1))
            uw = [_dot(xinv[t], rhs[t]) for t in range(n)]
            yield
            glast = [g[DN_CHUNK - 1:DN_CHUNK, :] for g in gc]
            kuw = [_dot_tn(k2[t] * jnp.exp(glast[t] - gc[t]), uw[t]) for t in range(n)]
            auw = [_dot(amat[t], jnp.concatenate([bd(uw[t][:, 0:GW]), bd(uw[t][:, GW:2 * GW])], axis=1))
                   for t in range(n)]
            yield
            for t in range(n):
                rows, lanes, gidx = batch[t]
                m2 = eye2 * jnp.exp(glast[t]) - pack_diag(kuw[t][:, GW:2 * GW])
                r2 = pack_diag(kuw[t][:, 0:GW])
                qeff = q2[t] * eg[t] - auw[t][:, GW:2 * GW]
                local.append((rows, lanes, gidx, m2, r2, qeff, auw[t][:, 0:GW]))
            yield

    def chain_stream():
        states = [s_ref[gidx] for gidx in range(NG)]
        outs = []
        for rows, lanes, gidx, m2, r2, qeff, oloc in local:
            res = _dot(jnp.concatenate([qeff, m2], axis=0), bd(states[gidx]))
            outs.append(oloc + res[0:DN_CHUNK])
            states[gidx] = res[DN_CHUNK:2 * DN_CHUNK] + r2
            if gidx == NG - 1:
                od = jnp.concatenate(outs, axis=1)
                outs = []
                mix_ref[rows, SWA_WIDTH:SWA_WIDTH + DN_WIDTH] = (
                    od * lax.rsqrt(_dot(od * od, head_ones) * (1.0 / HEAD_DIM) + EPS) * dn_g_ref[...])
                yield
        for gidx in range(NG):
            s_ref[gidx] = states[gidx]

    def mem_stream():
        for r0 in range(0, ts, MEM_ROWS):
            rows = slice(r0, r0 + MEM_ROWS)
            for p in range(MEM_HEADS // 2):
                q2 = mq_ref[rows, p * LANES:(p + 1) * LANES]
                scores = [_dot_nt(q2, mkv_ref[p * 2 + half]) for half in range(2)]
                yield
                acc = None
                for half in range(2):
                    s = scores[half]
                    e = jnp.exp(s - jnp.max(s, axis=-1, keepdims=True))
                    o = _dot(e, mkv_ref[4 + p * 2 + half]) * (1.0 / jnp.sum(e, axis=-1, keepdims=True))
                    acc = o if acc is None else acc + o
                    yield
                off = SWA_WIDTH + DN_WIDTH + p * LANES
                mix_ref[rows, off:off + LANES] = acc

    _round_robin([proj_stream(), _delayed(CONV_DELAY, conv_stream()), _delayed(SWA_DELAY, swa_stream()),
                  _delayed(GDN_DELAY, gdn_stream()), _delayed(MEM_DELAY, mem_stream())])

    def out_stream(r0):
        rows = slice(r0, r0 + OUT_ROWS)
        mixed = (mix_ref[rows, :] * _silu(gate_ref[rows, :])).astype(BF16)
        yield
        y = jnp.dot(mixed, w_out_ref[...], preferred_element_type=F32)
        yield
        out_ref[0, rows, :] = x_ref[0, rows, :] + _rms(y, post_g_ref[...])

    chunks_per_unit = OUT_ROWS // DN_CHUNK
    _round_robin([chain_stream()] + [_delayed(chunks_per_unit * (k + 1), out_stream(r0))
                                     for k, r0 in enumerate(range(0, ts, OUT_ROWS))])


def _layer(x, mem, sinks, pre_g, w_in, conv_w, alog, dtb, dn_g, mem_g, w_mem, w_out, post_g):
    b, s, d = x.shape
    ts = min(SEQ_TILE, s)
    assert s % ts == 0 and ts % WINDOW == 0 and d == D_MODEL
    const = lambda bi, si: (0, 0)
    return pl.pallas_call(
        functools.partial(_layer_kernel, ts=ts),
        out_shape=jax.ShapeDtypeStruct(x.shape, x.dtype),
        grid=(b, s // ts),
        in_specs=[
            pl.BlockSpec(memory_space=pltpu.SMEM),
            pl.BlockSpec((1, ts, d), lambda bi, si: (bi, si, 0)),
            pl.BlockSpec((1, N_MEM, d), lambda bi, si: (bi, 0, 0)),
            pl.BlockSpec((1, d), const),
            pl.BlockSpec((d, IN_COLS), const),
            pl.BlockSpec((DN_CONV, 3 * DN_WIDTH), const),
            pl.BlockSpec((1, LANES), const),
            pl.BlockSpec((1, LANES), const),
            pl.BlockSpec((1, DN_WIDTH), const),
            pl.BlockSpec((1, d), const),
            pl.BlockSpec((d, 2 * MEM_WIDTH), const),
            pl.BlockSpec((MIX_WIDTH, d), const),
            pl.BlockSpec((1, d), const),
        ],
        out_specs=pl.BlockSpec((1, ts, d), lambda bi, si: (bi, si, 0)),
        scratch_shapes=[
            pltpu.VMEM((ts, d), BF16),
            pltpu.VMEM((ts, SWA_WIDTH), BF16),
            pltpu.VMEM((4, WINDOW + ts, LANES), BF16),
            pltpu.VMEM((4, WINDOW + ts, LANES), BF16),
            pltpu.VMEM((CONV_HIST + ts, 3 * DN_WIDTH), F32),
            pltpu.VMEM((ts, DN_WIDTH), F32),
            pltpu.VMEM((ts, DN_WIDTH), F32),
            pltpu.VMEM((ts, DN_WIDTH), F32),
            pltpu.VMEM((ts, DN_WIDTH), F32),
            pltpu.VMEM((ts, DN_WIDTH), F32),
            pltpu.VMEM((NG, DN_CHUNK, GW), F32),
            pltpu.VMEM((8, N_MEM, LANES), BF16),
            pltpu.VMEM((ts, MEM_WIDTH), BF16),
            pltpu.VMEM((ts, MIX_WIDTH), F32),
            pltpu.VMEM((ts, MIX_WIDTH), F32),
        ],
        compiler_params=pltpu.CompilerParams(
            dimension_semantics=("arbitrary", "arbitrary"),
            vmem_limit_bytes=VMEM_LIMIT_BYTES),
        name="hybrid_layer",
    )(sinks, x, mem, pre_g, w_in, conv_w, alog, dtb, dn_g, mem_g, w_mem, w_out, post_g)


def _reorder_w_in(w):
    o = 0
    parts = {}
    for name, width in (("sq", 512), ("sk", 128), ("sv", 128), ("dq", 256), ("dk", 256), ("dv", 256),
                        ("beta", 4), ("alpha", 4), ("mq", 256), ("gate", 1024)):
        parts[name] = w[:, o:o + width]
        o += width
    pad = jnp.zeros((w.shape[0], LANES - 2 * DN_HEADS), w.dtype)
    return jnp.concatenate([parts[n] for n in ("sq", "sk", "sv", "dq", "dk", "dv", "mq", "gate",
                                               "beta", "alpha")] + [pad], axis=1).astype(BF16)


def _lane_place(v, start):
    return jnp.zeros((1, LANES), F32).at[0, start:start + v.shape[0]].set(v.astype(F32))


@jax.jit
def kernel(x, mem, pre_norm_g, w_in, conv_w, a_log, dt_bias, sinks, dn_norm_g, mem_norm_g, w_mem_kv,
           w_out, post_norm_g):
    depth = w_in.shape[0]
    for l in range(depth):
        x = _layer(
            x, mem, sinks[l].astype(F32),
            pre_norm_g[l][None, :], _reorder_w_in(w_in[l]), conv_w[l],
            _lane_place(a_log[l], DN_HEADS), _lane_place(dt_bias[l], DN_HEADS),
            jnp.tile(dn_norm_g[l], DN_HEADS)[None, :], mem_norm_g[l][None, :],
            w_mem_kv[l].astype(BF16), w_out[l].astype(BF16), post_norm_g[l][None, :])
    return x
```

```python
import functools

import jax
import jax.numpy as jnp
from jax import lax
from jax.experimental import pallas as pl
from jax.experimental.pallas import tpu as pltpu

F32 = jnp.float32
BF16 = jnp.bfloat16

D_MODEL = 1024
HEAD_DIM = 64
SWA_HEADS = 8
SWA_KV_HEADS = 2
SWA_WIDTH = SWA_HEADS * HEAD_DIM
WINDOW = 128
DN_HEADS = 4
DN_WIDTH = DN_HEADS * HEAD_DIM
DN_CONV = 4
DN_CHUNK = 64
N_MEM = 256
MEM_HEADS = 4
MEM_WIDTH = MEM_HEADS * HEAD_DIM
MIX_WIDTH = 1024
EPS = 1e-6
SCALE = HEAD_DIM ** -0.5
LOG2E = 1.4426950408889634

LANES = 128
MXU_N = 256
CONV_HIST = 8

C_SWA = 0
C_DN = 768
C_MQ = 1536
C_GATE = 1792
IN_COLS = 2816
BA_ROWS = 16

GW = 128
NG = DN_WIDTH // GW
HPG = GW // HEAD_DIM

SEQ_TILE = 512
GDN_BATCH = 8
ROW_BLK = 256
MEM_ROWS = ROW_BLK
OUT_ROWS = 256
PROJ_STEPS_PER_BLK = 8
CONV_DELAY = 3
SWA_DELAY = 6
GDN_DELAY = 7
MEM_DELAY = 10
VMEM_LIMIT_BYTES = 56 * 1024 * 1024


def _dot(a, b):
    return jnp.dot(a.astype(BF16), b.astype(BF16), preferred_element_type=F32)


def _dot_nt(a, b):
    return lax.dot_general(a.astype(BF16), b.astype(BF16), (((1,), (1,)), ((), ())),
                           preferred_element_type=F32)


def _dot_tn(a, b):
    return lax.dot_general(a.astype(BF16), b.astype(BF16), (((0,), (0,)), ((), ())),
                           preferred_element_type=F32)


def _split2(x):
    x0 = x.astype(BF16)
    x1 = (x - x0.astype(F32)).astype(BF16)
    return x0, x1


def _sel_dot(m, x):
    x0, x1 = _split2(x)
    return jnp.dot(m, x0, preferred_element_type=F32) + jnp.dot(m, x1, preferred_element_type=F32)


def _iota(shape, dim):
    return lax.broadcasted_iota(jnp.int32, shape, dim)


def _rms(x, g):
    return x * lax.rsqrt(jnp.mean(x * x, axis=-1, keepdims=True) + EPS) * g


def _silu(x):
    hx = 0.5 * x
    return hx + hx * jnp.tanh(hx)


def _round_robin(streams):
    streams = list(streams)
    while streams:
        for s in list(streams):
            try:
                next(s)
            except StopIteration:
                streams.remove(s)


def _delayed(n, gen):
    for _ in range(n):
        yield
    yield from gen


def _layer_kernel(sinks_ref, x_ref, mem_ref, pre_g_ref, w_in_ref, w_ba_ref, conv_w_ref, alog_ref, dtb_ref,
                  dn_g_ref, mem_g_ref, w_mem_ref, w_out_ref, post_g_ref, out_ref,
                  h_ref, sq_ref, kk_ref, vv_ref, dnp_ref, qn_ref, kn_ref, vn_ref, gb_ref, bb_ref,
                  s_ref, mkv_ref, mq_ref, gate_ref, mix_ref, *, ts):
    i = pl.program_id(1)
    nblk = ts // WINDOW
    nchunk = ts // DN_CHUNK

    lane = _iota((1, LANES), 1)
    lo = lane < HEAD_DIM
    emitted = set()

    @pl.when(i == 0)
    def _():
        s_ref[...] = jnp.zeros_like(s_ref)
        kk_ref[:, 0:WINDOW, :] = jnp.zeros((4, WINDOW, LANES), BF16)
        vv_ref[:, 0:WINDOW, :] = jnp.zeros((4, WINDOW, LANES), BF16)
        dnp_ref[0:CONV_HIST, :] = jnp.zeros((CONV_HIST, 3 * DN_WIDTH), F32)
        hm = _rms(mem_ref[0], mem_g_ref[...])
        mkv = _dot(hm, w_mem_ref[...])
        for kind in range(2):
            for p in range(MEM_HEADS // 2):
                blk = mkv[:, kind * MEM_WIDTH + p * LANES: kind * MEM_WIDTH + (p + 1) * LANES]
                mkv_ref[kind * 4 + p * 2 + 0] = jnp.where(lo, blk, 0.0).astype(BF16)
                mkv_ref[kind * 4 + p * 2 + 1] = jnp.where(lo, 0.0, blk).astype(BF16)

    h_ref[...] = _rms(x_ref[0], pre_g_ref[...]).astype(BF16)

    def proj(c0, width, rows):
        return jnp.dot(h_ref[rows, :], w_in_ref[:, c0:c0 + width], preferred_element_type=F32)

    sel_r = _iota((BA_ROWS, DN_WIDTH), 0)
    sel_c = _iota((BA_ROWS, DN_WIDTH), 1) // HEAD_DIM
    beta_sel = (sel_r == sel_c).astype(BF16)
    decay_sel = (sel_r == DN_HEADS + sel_c).astype(BF16)

    def proj_stream():
        for r0 in range(0, ts, ROW_BLK):
            rows = slice(r0, r0 + ROW_BLK)
            hist_rows = slice(CONV_HIST + r0, CONV_HIST + r0 + ROW_BLK)
            kv_rows = slice(WINDOW + r0, WINDOW + r0 + ROW_BLK)
            for c in range(3 * DN_WIDTH // MXU_N):
                dnp_ref[hist_rows, c * MXU_N:(c + 1) * MXU_N] = proj(C_DN + c * MXU_N, MXU_N, rows)
                emitted.add(("dn", r0, c))
                yield
            for c in range(SWA_WIDTH // MXU_N):
                sq_ref[rows, c * MXU_N:(c + 1) * MXU_N] = proj(C_SWA + c * MXU_N, MXU_N, rows).astype(BF16)
                yield
            kvcur = proj(C_SWA + SWA_WIDTH, 2 * LANES, rows)
            for ref, cur in ((kk_ref, kvcur[:, 0:LANES]), (vv_ref, kvcur[:, LANES:2 * LANES])):
                rolled = pltpu.roll(cur, HEAD_DIM, 1)
                ref[0, kv_rows, :] = jnp.where(lo, cur, 0.0).astype(BF16)
                ref[1, kv_rows, :] = jnp.where(lo, 0.0, rolled).astype(BF16)
                ref[2, kv_rows, :] = jnp.where(lo, rolled, 0.0).astype(BF16)
                ref[3, kv_rows, :] = jnp.where(lo, 0.0, cur).astype(BF16)
            emitted.add(("qkv", r0))
            yield
            ba = _dot_nt(w_ba_ref[...], h_ref[rows, :])
            beta = 1.0 / (1.0 + jnp.exp(-ba))
            z = ba + dtb_ref[...]
            softplus = jnp.maximum(z, 0.0) + jnp.log(1.0 + jnp.exp(-jnp.abs(z)))
            glog = -jnp.exp(alog_ref[...]) * softplus
            bb_ref[rows, :] = _dot_tn(beta, beta_sel)
            g0, g1 = _split2(glog)
            gb_ref[rows, :] = _dot_tn(g0, decay_sel) + _dot_tn(g1, decay_sel)
            emitted.add(("ba", r0))
            yield
            mq_ref[rows, :] = proj(C_MQ, MEM_WIDTH, rows).astype(BF16)
            emitted.add(("mq", r0))
            yield
        for r0 in range(0, ts, ROW_BLK):
            rows = slice(r0, r0 + ROW_BLK)
            for c in range(MIX_WIDTH // MXU_N):
                gate_ref[rows, c * MXU_N:(c + 1) * MXU_N] = proj(C_GATE + c * MXU_N, MXU_N, rows)
                yield

    qi = _iota((WINDOW, 2 * WINDOW), 0)
    kj = _iota((WINDOW, 2 * WINDOW), 1)
    band = (kj > qi) & (kj <= qi + WINDOW)
    band0 = band & ((kj >= WINDOW) | (i > 0))
    upper = _iota((2 * WINDOW, 1), 0) < WINDOW
    group = SWA_HEADS // SWA_KV_HEADS

    def swa_stream(blocks):
        for j in blocks:
            assert ("qkv", j * WINDOW // ROW_BLK * ROW_BLK) in emitted
            rows = slice(j * WINDOW, (j + 1) * WINDOW)
            krows = slice(j * WINDOW, (j + 2) * WINDOW)
            valid = band0 if j == 0 else band
            valid = jnp.concatenate([valid, valid], axis=0)
            for kvh in range(SWA_KV_HEADS):
                p0 = kvh * (group // 2)
                q_st = jnp.concatenate([sq_ref[rows, p0 * LANES:(p0 + 1) * LANES],
                                        sq_ref[rows, (p0 + 1) * LANES:(p0 + 2) * LANES]], axis=0)
                acc = None
                for half in range(2):
                    s = _dot_nt(q_st, kk_ref[2 * kvh + half, krows, :])
                    yield
                    s = jnp.where(valid, s, -jnp.inf)
                    sink = jnp.where(upper, sinks_ref[2 * p0 + half], sinks_ref[2 * p0 + 2 + half]) * LOG2E
                    m = jnp.maximum(jnp.max(s, axis=-1, keepdims=True), sink)
                    e = jnp.exp2(s - m)
                    denom = jnp.sum(e, axis=-1, keepdims=True) + jnp.exp2(sink - m)
                    o = _dot(e, vv_ref[2 * kvh + half, krows, :]) * (1.0 / denom)
                    acc = o if acc is None else acc + o
                    yield
                mix_ref[rows, p0 * LANES:(p0 + 1) * LANES] = acc[0:WINDOW]
                mix_ref[rows, (p0 + 1) * LANES:(p0 + 2) * LANES] = acc[WINDOW:2 * WINDOW]

    head_ones = ((_iota((DN_WIDTH, DN_WIDTH), 0) // HEAD_DIM)
                 == (_iota((DN_WIDTH, DN_WIDTH), 1) // HEAD_DIM)).astype(BF16)

    def conv_stream():
        for r0 in range(0, ts, ROW_BLK):
            rows = slice(r0, r0 + ROW_BLK)
            for part, dst in enumerate((qn_ref, kn_ref, vn_ref)):
                cols = slice(part * DN_WIDTH, (part + 1) * DN_WIDTH)
                assert ("dn", r0, part) in emitted
                xs = dnp_ref[r0:r0 + CONV_HIST + ROW_BLK, cols]
                acc = xs * conv_w_ref[0:1, cols]
                for j in range(1, DN_CONV):
                    acc = pltpu.roll(acc, 1, 0) + xs * conv_w_ref[j:j + 1, cols]
                conv = _silu(acc[CONV_HIST:CONV_HIST + ROW_BLK, :])
                if part == 0:
                    dst[rows, :] = conv * (lax.rsqrt(_dot(conv * conv, head_ones) + EPS) * SCALE)
                elif part == 1:
                    dst[rows, :] = conv * lax.rsqrt(_dot(conv * conv, head_ones) + EPS)
                else:
                    dst[rows, :] = conv
                yield
            emitted.add(("conv", r0))
            for _ in range(PROJ_STEPS_PER_BLK - 3):
                yield
        dnp_ref[0:CONV_HIST, :] = dnp_ref[ts:ts + CONV_HIST, :]

    cc = _iota((DN_CHUNK, GW), 0)
    cs = _iota((DN_CHUNK, GW), 1) % HEAD_DIM
    causal = cc >= cs
    strict = cc > cs
    diag = cc == cs
    eye2 = diag.astype(F32)
    lo_g = _iota((DN_CHUNK, GW), 1) < HEAD_DIM
    tri = (_iota((DN_CHUNK, DN_CHUNK), 0) >= _iota((DN_CHUNK, DN_CHUNK), 1)).astype(BF16)
    bd_mask = ((_iota((GW, GW), 0) // HEAD_DIM) == (_iota((GW, GW), 1) // HEAD_DIM)).astype(BF16)

    def bd(x):
        xb = x.astype(BF16)
        return jnp.concatenate([xb] * HPG, axis=0) * bd_mask

    def pack_diag(x):
        return jnp.where(lo_g, x[0:HEAD_DIM], x[HEAD_DIM:2 * HEAD_DIM])

    items = [(slice(c * DN_CHUNK, (c + 1) * DN_CHUNK), slice(gidx * GW, (gidx + 1) * GW), gidx)
             for c in range(nchunk) for gidx in range(NG)]

    local = []

    def gdn_stream():
        for b0 in range(0, len(items), GDN_BATCH):
            batch = items[b0:b0 + GDN_BATCH]
            n = len(batch)
            for rows, _, _ in batch:
                rb0 = rows.start // ROW_BLK * ROW_BLK
                assert ("conv", rb0) in emitted and ("ba", rb0) in emitted
            q2 = [qn_ref[rows, lanes] for rows, lanes, _ in batch]
            k2 = [kn_ref[rows, lanes] for rows, lanes, _ in batch]
            beta2 = [bb_ref[rows, lanes] for rows, lanes, _ in batch]
            gcs = {}
            for rows, _, _ in batch:
                if rows.start not in gcs:
                    gcs[rows.start] = _sel_dot(tri, gb_ref[rows, :])
            gc = [gcs[rows.start][:, lanes] for rows, lanes, _ in batch]
            kb2 = [k2[t] * beta2[t] for t in range(n)]
            kq = [_dot_nt(jnp.concatenate([kb2[t], q2[t]], axis=0), bd(k2[t])) for t in range(n)]
            yield
            grow = [jnp.sum(jnp.where(diag, g, 0.0), axis=0, keepdims=True) for g in gc]
            dec = [jnp.exp(jnp.where(causal, gc[t] - grow[t], -jnp.inf)) for t in range(n)]
            nmat = [jnp.where(strict, -(kq[t][0:DN_CHUNK] * dec[t]), 0.0) for t in range(n)]
            amat = [kq[t][DN_CHUNK:2 * DN_CHUNK] * dec[t] for t in range(n)]
            xinv = [eye2 + m for m in nmat]
            pw = [_dot(m, bd(m)) for m in nmat]
            yield
            for r in range(1, 6):
                if r < 5:
                    res = [_dot(pw[t], jnp.concatenate([bd(pw[t]), bd(xinv[t])], axis=1)) for t in range(n)]
                    pw = [x[:, 0:GW] for x in res]
                    xinv = [xinv[t] + res[t][:, GW:2 * GW] for t in range(n)]
                else:
                    xinv = [xinv[t] + _dot(pw[t], bd(xinv[t])) for t in range(n)]
                yield
            eg = [jnp.exp(g) for g in gc]
            rhs = []
            for t in range(n):
                rows, lanes, _ = batch[t]
                rhs.append(jnp.concatenate([bd(vn_ref[rows, lanes] * beta2[t]), bd(kb2[t] * eg[t])], axis=1))
            uw = [_dot(xinv[t], rhs[t]) for t in range(n)]
            yield
            glast = [g[DN_CHUNK - 1:DN_CHUNK, :] for g in gc]
            kuw = [_dot_tn(k2[t] * jnp.exp(glast[t] - gc[t]), uw[t]) for t in range(n)]
            auw = [_dot(amat[t], jnp.concatenate([bd(uw[t][:, 0:GW]), bd(uw[t][:, GW:2 * GW])], axis=1))
                   for t in range(n)]
            yield
            for t in range(n):
                rows, lanes, gidx = batch[t]
                m2 = eye2 * jnp.exp(glast[t]) - pack_diag(kuw[t][:, GW:2 * GW])
                r2 = pack_diag(kuw[t][:, 0:GW])
                qeff = q2[t] * eg[t] - auw[t][:, GW:2 * GW]
                local.append((rows, lanes, gidx, m2, r2, qeff, auw[t][:, 0:GW]))
            yield

    def chain_stream():
        states = [s_ref[gidx] for gidx in range(NG)]
        outs = []
        for rows, lanes, gidx, m2, r2, qeff, oloc in local:
            res = _dot(jnp.concatenate([qeff, m2], axis=0), bd(states[gidx]))
            outs.append(oloc + res[0:DN_CHUNK])
            states[gidx] = res[DN_CHUNK:2 * DN_CHUNK] + r2
            if gidx == NG - 1:
                od = jnp.concatenate(outs, axis=1)
                outs = []
                mix_ref[rows, SWA_WIDTH:SWA_WIDTH + DN_WIDTH] = (
                    od * lax.rsqrt(_dot(od * od, head_ones) * (1.0 / HEAD_DIM) + EPS) * dn_g_ref[...])
                yield
        for gidx in range(NG):
            s_ref[gidx] = states[gidx]

    def mem_stream():
        for r0 in range(0, ts, MEM_ROWS):
            rows = slice(r0, r0 + MEM_ROWS)
            assert ("mq", r0) in emitted
            for p in range(MEM_HEADS // 2):
                q2 = mq_ref[rows, p * LANES:(p + 1) * LANES]
                acc = None
                for half in range(2):
                    s = _dot_nt(q2, mkv_ref[p * 2 + half])
                    yield
                    e = jnp.exp2(s - jnp.max(s, axis=-1, keepdims=True))
                    o = _dot(e, mkv_ref[4 + p * 2 + half]) * (1.0 / jnp.sum(e, axis=-1, keepdims=True))
                    acc = o if acc is None else acc + o
                    yield
                off = SWA_WIDTH + DN_WIDTH + p * LANES
                mix_ref[rows, off:off + LANES] = acc

    blk_per_rb = ROW_BLK // WINDOW
    swa_streams = [_delayed(SWA_DELAY + rb * PROJ_STEPS_PER_BLK,
                            swa_stream(range(rb * blk_per_rb, (rb + 1) * blk_per_rb)))
                   for rb in range(ts // ROW_BLK)]
    _round_robin([proj_stream(), _delayed(CONV_DELAY, conv_stream())] + swa_streams
                 + [_delayed(GDN_DELAY, gdn_stream()), _delayed(MEM_DELAY, mem_stream())])
    for ref in (kk_ref, vv_ref):
        ref[:, 0:WINDOW, :] = ref[:, ts:ts + WINDOW, :]

    def out_stream(r0):
        rows = slice(r0, r0 + OUT_ROWS)
        mixed = (mix_ref[rows, :] * _silu(gate_ref[rows, :])).astype(BF16)
        yield
        y = jnp.dot(mixed, w_out_ref[...], preferred_element_type=F32)
        yield
        out_ref[0, rows, :] = x_ref[0, rows, :] + _rms(y, post_g_ref[...])

    chunks_per_unit = OUT_ROWS // DN_CHUNK
    _round_robin([chain_stream()] + [_delayed(chunks_per_unit * (k + 1), out_stream(r0))
                                     for k, r0 in enumerate(range(0, ts, OUT_ROWS))])


def _layer(x, mem, sinks, pre_g, w_in, w_ba, conv_w, alog, dtb, dn_g, mem_g, w_mem, w_out, post_g):
    b, s, d = x.shape
    ts = min(SEQ_TILE, s)
    assert s % ts == 0 and ts % ROW_BLK == 0 and d == D_MODEL
    const = lambda bi, si: (0, 0)
    return pl.pallas_call(
        functools.partial(_layer_kernel, ts=ts),
        out_shape=jax.ShapeDtypeStruct(x.shape, x.dtype),
        grid=(b, s // ts),
        in_specs=[
            pl.BlockSpec(memory_space=pltpu.SMEM),
            pl.BlockSpec((1, ts, d), lambda bi, si: (bi, si, 0)),
            pl.BlockSpec((1, N_MEM, d), lambda bi, si: (bi, 0, 0)),
            pl.BlockSpec((1, d), const),
            pl.BlockSpec((d, IN_COLS), const),
            pl.BlockSpec((BA_ROWS, d), const),
            pl.BlockSpec((DN_CONV, 3 * DN_WIDTH), const),
            pl.BlockSpec((BA_ROWS, 1), const),
            pl.BlockSpec((BA_ROWS, 1), const),
            pl.BlockSpec((1, DN_WIDTH), const),
            pl.BlockSpec((1, d), const),
            pl.BlockSpec((d, 2 * MEM_WIDTH), const),
            pl.BlockSpec((MIX_WIDTH, d), const),
            pl.BlockSpec((1, d), const),
        ],
        out_specs=pl.BlockSpec((1, ts, d), lambda bi, si: (bi, si, 0)),
        scratch_shapes=[
            pltpu.VMEM((ts, d), BF16),
            pltpu.VMEM((ts, SWA_WIDTH), BF16),
            pltpu.VMEM((4, WINDOW + ts, LANES), BF16),
            pltpu.VMEM((4, WINDOW + ts, LANES), BF16),
            pltpu.VMEM((CONV_HIST + ts, 3 * DN_WIDTH), F32),
            pltpu.VMEM((ts, DN_WIDTH), F32),
            pltpu.VMEM((ts, DN_WIDTH), F32),
            pltpu.VMEM((ts, DN_WIDTH), F32),
            pltpu.VMEM((ts, DN_WIDTH), F32),
            pltpu.VMEM((ts, DN_WIDTH), F32),
            pltpu.VMEM((NG, DN_CHUNK, GW), F32),
            pltpu.VMEM((8, N_MEM, LANES), BF16),
            pltpu.VMEM((ts, MEM_WIDTH), BF16),
            pltpu.VMEM((ts, MIX_WIDTH), F32),
            pltpu.VMEM((ts, MIX_WIDTH), F32),
        ],
        compiler_params=pltpu.CompilerParams(
            dimension_semantics=("arbitrary", "arbitrary"),
            vmem_limit_bytes=VMEM_LIMIT_BYTES),
        name="hybrid_layer",
    )(sinks, x, mem, pre_g, w_in, w_ba, conv_w, alog, dtb, dn_g, mem_g, w_mem, w_out, post_g)


def _reorder_w_in(w):
    o = 0
    parts = {}
    for name, width in (("sq", 512), ("sk", 128), ("sv", 128), ("dq", 256), ("dk", 256), ("dv", 256),
                        ("beta", 4), ("alpha", 4), ("mq", 256), ("gate", 1024)):
        parts[name] = w[:, o:o + width]
        o += width
    for name in ("sq", "mq"):
        parts[name] = parts[name] * (SCALE * LOG2E)
    main = jnp.concatenate([parts[n] for n in ("sq", "sk", "sv", "dq", "dk", "dv", "mq", "gate")], axis=1)
    pad = jnp.zeros((BA_ROWS - 2 * DN_HEADS, w.shape[0]), w.dtype)
    w_ba = jnp.concatenate([parts["beta"].T, parts["alpha"].T, pad], axis=0)
    return main.astype(BF16), w_ba.astype(BF16)


def _row_place(v, start):
    return jnp.zeros((BA_ROWS, 1), F32).at[start:start + v.shape[0], 0].set(v.astype(F32))


@jax.jit
def kernel(x, mem, pre_norm_g, w_in, conv_w, a_log, dt_bias, sinks, dn_norm_g, mem_norm_g, w_mem_kv,
           w_out, post_norm_g):
    depth = w_in.shape[0]
    for l in range(depth):
        x = _layer(
            x, mem, sinks[l].astype(F32),
            pre_norm_g[l][None, :], *_reorder_w_in(w_in[l]), conv_w[l],
            _row_place(a_log[l], DN_HEADS), _row_place(dt_bias[l], DN_HEADS),
            jnp.tile(dn_norm_g[l], DN_HEADS)[None, :], mem_norm_g[l][None, :],
            w_mem_kv[l].astype(BF16), w_out[l].astype(BF16), post_norm_g[l][None, :])
    return x
```

```python
import functools

import jax
import jax.numpy as jnp
from jax import lax
from jax.experimental import pallas as pl
from jax.experimental.pallas import tpu as pltpu

F32 = jnp.float32
BF16 = jnp.bfloat16

D_MODEL = 1024
HEAD_DIM = 64
SWA_HEADS = 8
SWA_KV_HEADS = 2
SWA_WIDTH = SWA_HEADS * HEAD_DIM
WINDOW = 128
DN_HEADS = 4
DN_WIDTH = DN_HEADS * HEAD_DIM
DN_CONV = 4
DN_CHUNK = 64
N_MEM = 256
MEM_HEADS = 4
MEM_WIDTH = MEM_HEADS * HEAD_DIM
MIX_WIDTH = 1024
EPS = 1e-6
SCALE = HEAD_DIM ** -0.5
LOG2E = 1.4426950408889634

LANES = 128
MXU_N = 256
CONV_HIST = 8

C_SWA = 0
C_DN = 768
C_MQ = 1536
C_GATE = 1792
IN_COLS = 2816
BA_ROWS = 16

GW = 128
NG = DN_WIDTH // GW
HPG = GW // HEAD_DIM

SEQ_TILE = 512
GDN_BATCH = 8
ROW_BLK = 256
MEM_ROWS = ROW_BLK
OUT_ROWS = 256
PROJ_STEPS_PER_BLK = 8
CONV_DELAY = 3
SWA_DELAY = 6
GDN_DELAY = 7
MEM_DELAY = 10
VMEM_LIMIT_BYTES = 56 * 1024 * 1024


def _dot(a, b):
    return jnp.dot(a.astype(BF16), b.astype(BF16), preferred_element_type=F32)


def _dot_nt(a, b):
    return lax.dot_general(a.astype(BF16), b.astype(BF16), (((1,), (1,)), ((), ())),
                           preferred_element_type=F32)


def _dot_tn(a, b):
    return lax.dot_general(a.astype(BF16), b.astype(BF16), (((0,), (0,)), ((), ())),
                           preferred_element_type=F32)


def _split2(x):
    x0 = x.astype(BF16)
    x1 = (x - x0.astype(F32)).astype(BF16)
    return x0, x1


def _sel_dot(m, x):
    x0, x1 = _split2(x)
    return jnp.dot(m, x0, preferred_element_type=F32) + jnp.dot(m, x1, preferred_element_type=F32)


def _iota(shape, dim):
    return lax.broadcasted_iota(jnp.int32, shape, dim)


def _rms(x, g):
    return x * lax.rsqrt(jnp.mean(x * x, axis=-1, keepdims=True) + EPS) * g


def _silu(x):
    hx = 0.5 * x
    return hx + hx * jnp.tanh(hx)


def _round_robin(streams):
    streams = list(streams)
    while streams:
        for s in list(streams):
            try:
                next(s)
            except StopIteration:
                streams.remove(s)


def _delayed(n, gen):
    for _ in range(n):
        yield
    yield from gen


def _layer_kernel(sinks_ref, x_ref, mem_ref, pre_g_ref, w_in_ref, w_ba_ref, conv_w_ref, alog_ref, dtb_ref,
                  dn_g_ref, mem_g_ref, w_mem_ref, w_out_ref, post_g_ref, out_ref,
                  h_ref, sq_ref, kk_ref, vv_ref, dnp_ref, qn_ref, kn_ref, vn_ref, gb_ref, bb_ref,
                  s_ref, mkv_ref, mq_ref, gate_ref, mix_ref, *, ts, layer):
    i = pl.program_id(1)
    nblk = ts // WINDOW
    nchunk = ts // DN_CHUNK

    lane = _iota((1, LANES), 1)
    lo = lane < HEAD_DIM
    emitted = set()

    @pl.when(i == 0)
    def _():
        s_ref[...] = jnp.zeros_like(s_ref)
        kk_ref[:, 0:WINDOW, :] = jnp.zeros((4, WINDOW, LANES), BF16)
        vv_ref[:, 0:WINDOW, :] = jnp.zeros((4, WINDOW, LANES), BF16)
        dnp_ref[0:CONV_HIST, :] = jnp.zeros((CONV_HIST, 3 * DN_WIDTH), F32)
        hm = _rms(mem_ref[0], mem_g_ref[...])
        mkv = _dot(hm, w_mem_ref[...])
        for kind in range(2):
            for p in range(MEM_HEADS // 2):
                blk = mkv[:, kind * MEM_WIDTH + p * LANES: kind * MEM_WIDTH + (p + 1) * LANES]
                mkv_ref[kind * 4 + p * 2 + 0] = jnp.where(lo, blk, 0.0).astype(BF16)
                mkv_ref[kind * 4 + p * 2 + 1] = jnp.where(lo, 0.0, blk).astype(BF16)

    h_ref[...] = _rms(x_ref[0], pre_g_ref[...]).astype(BF16)

    def proj(c0, width, rows):
        return jnp.dot(h_ref[rows, :], w_in_ref[:, c0:c0 + width], preferred_element_type=F32)

    sel_r = _iota((BA_ROWS, DN_WIDTH), 0)
    sel_c = _iota((BA_ROWS, DN_WIDTH), 1) // HEAD_DIM
    beta_sel = (sel_r == sel_c).astype(BF16)
    decay_sel = (sel_r == DN_HEADS + sel_c).astype(BF16)

    def proj_stream():
        for r0 in range(0, ts, ROW_BLK):
            rows = slice(r0, r0 + ROW_BLK)
            hist_rows = slice(CONV_HIST + r0, CONV_HIST + r0 + ROW_BLK)
            kv_rows = slice(WINDOW + r0, WINDOW + r0 + ROW_BLK)
            for c in range(3 * DN_WIDTH // MXU_N):
                dnp_ref[hist_rows, c * MXU_N:(c + 1) * MXU_N] = proj(C_DN + c * MXU_N, MXU_N, rows)
                emitted.add(("dn", r0, c))
                yield
            for c in range(SWA_WIDTH // MXU_N):
                sq_ref[rows, c * MXU_N:(c + 1) * MXU_N] = proj(C_SWA + c * MXU_N, MXU_N, rows).astype(BF16)
                yield
            kvcur = proj(C_SWA + SWA_WIDTH, 2 * LANES, rows)
            for ref, cur in ((kk_ref, kvcur[:, 0:LANES]), (vv_ref, kvcur[:, LANES:2 * LANES])):
                rolled = pltpu.roll(cur, HEAD_DIM, 1)
                ref[0, kv_rows, :] = jnp.where(lo, cur, 0.0).astype(BF16)
                ref[1, kv_rows, :] = jnp.where(lo, 0.0, rolled).astype(BF16)
                ref[2, kv_rows, :] = jnp.where(lo, rolled, 0.0).astype(BF16)
                ref[3, kv_rows, :] = jnp.where(lo, 0.0, cur).astype(BF16)
            emitted.add(("qkv", r0))
            yield
            ba = _dot_nt(w_ba_ref[...], h_ref[rows, :])
            beta = 1.0 / (1.0 + jnp.exp(-ba))
            z = ba + dtb_ref[...]
            softplus = jnp.maximum(z, 0.0) + jnp.log(1.0 + jnp.exp(-jnp.abs(z)))
            glog = -jnp.exp(alog_ref[...]) * softplus
            bb_ref[rows, :] = _dot_tn(beta, beta_sel)
            g0, g1 = _split2(glog)
            gb_ref[rows, :] = _dot_tn(g0, decay_sel) + _dot_tn(g1, decay_sel)
            emitted.add(("ba", r0))
            yield
            mq_ref[rows, :] = proj(C_MQ, MEM_WIDTH, rows).astype(BF16)
            emitted.add(("mq", r0))
            yield
        for r0 in range(0, ts, ROW_BLK):
            rows = slice(r0, r0 + ROW_BLK)
            for c in range(MIX_WIDTH // MXU_N):
                gate_ref[rows, c * MXU_N:(c + 1) * MXU_N] = proj(C_GATE + c * MXU_N, MXU_N, rows)
                yield

    from_prev = _iota((2 * WINDOW, WINDOW), 1) > (_iota((2 * WINDOW, WINDOW), 0) % WINDOW)
    prev_mask = from_prev.astype(BF16)
    cur_mask = 1.0 - prev_mask
    upper = _iota((2 * WINDOW, 1), 0) < WINDOW
    group = SWA_HEADS // SWA_KV_HEADS

    def swa_stream(blocks):
        for j in blocks:
            assert ("qkv", j * WINDOW // ROW_BLK * ROW_BLK) in emitted
            rows = slice(j * WINDOW, (j + 1) * WINDOW)
            krows = slice(j * WINDOW, (j + 2) * WINDOW)
            for kvh in range(SWA_KV_HEADS):
                p0 = kvh * (group // 2)
                q_st = jnp.concatenate([sq_ref[rows, p0 * LANES:(p0 + 1) * LANES],
                                        sq_ref[rows, (p0 + 1) * LANES:(p0 + 2) * LANES]], axis=0)
                acc = None
                for half in range(2):
                    s = _dot_nt(q_st, kk_ref[2 * kvh + half, krows, :])
                    yield
                    s_prev = s[:, 0:WINDOW]
                    if j == 0:
                        s_prev = jnp.where(i > 0, s_prev, -jnp.inf)
                    s = jnp.where(from_prev, s_prev, s[:, WINDOW:2 * WINDOW])
                    sink = jnp.where(upper, sinks_ref[layer, 2 * p0 + half],
                                     sinks_ref[layer, 2 * p0 + 2 + half]) * LOG2E
                    m = jnp.maximum(jnp.max(s, axis=-1, keepdims=True), sink)
                    e = jnp.exp2(s - m)
                    denom = jnp.sum(e, axis=-1, keepdims=True) + jnp.exp2(sink - m)
                    eb = e.astype(BF16)
                    e2 = jnp.concatenate([eb * prev_mask, eb * cur_mask], axis=1)
                    o = _dot(e2, vv_ref[2 * kvh + half, krows, :]) * (1.0 / denom)
                    acc = o if acc is None else acc + o
                    yield
                mix_ref[rows, p0 * LANES:(p0 + 1) * LANES] = acc[0:WINDOW]
                mix_ref[rows, (p0 + 1) * LANES:(p0 + 2) * LANES] = acc[WINDOW:2 * WINDOW]

    head_ones = ((_iota((DN_WIDTH, DN_WIDTH), 0) // HEAD_DIM)
                 == (_iota((DN_WIDTH, DN_WIDTH), 1) // HEAD_DIM)).astype(BF16)

    def conv_stream():
        for r0 in range(0, ts, ROW_BLK):
            rows = slice(r0, r0 + ROW_BLK)
            for part, dst in enumerate((qn_ref, kn_ref, vn_ref)):
                cols = slice(part * DN_WIDTH, (part + 1) * DN_WIDTH)
                assert ("dn", r0, part) in emitted
                xs = dnp_ref[r0:r0 + CONV_HIST + ROW_BLK, cols]
                acc = xs * conv_w_ref[0:1, cols]
                for j in range(1, DN_CONV):
                    acc = pltpu.roll(acc, 1, 0) + xs * conv_w_ref[j:j + 1, cols]
                conv = _silu(acc[CONV_HIST:CONV_HIST + ROW_BLK, :])
                if part == 0:
                    dst[rows, :] = conv * (lax.rsqrt(_dot(conv * conv, head_ones) + EPS) * SCALE)
                elif part == 1:
                    dst[rows, :] = conv * lax.rsqrt(_dot(conv * conv, head_ones) + EPS)
                else:
                    dst[rows, :] = conv
                yield
            emitted.add(("conv", r0))
            for _ in range(PROJ_STEPS_PER_BLK - 3):
                yield
        dnp_ref[0:CONV_HIST, :] = dnp_ref[ts:ts + CONV_HIST, :]

    cc = _iota((DN_CHUNK, GW), 0)
    cs = _iota((DN_CHUNK, GW), 1) % HEAD_DIM
    causal = cc >= cs
    strict = cc > cs
    diag = cc == cs
    eye2 = diag.astype(F32)
    lo_g = _iota((DN_CHUNK, GW), 1) < HEAD_DIM
    tri = (_iota((DN_CHUNK, DN_CHUNK), 0) >= _iota((DN_CHUNK, DN_CHUNK), 1)).astype(BF16)
    bd_mask = ((_iota((GW, GW), 0) // HEAD_DIM) == (_iota((GW, GW), 1) // HEAD_DIM)).astype(BF16)

    def bd(x):
        xb = x.astype(BF16)
        return jnp.concatenate([xb] * HPG, axis=0) * bd_mask

    def pack_diag(x):
        return jnp.where(lo_g, x[0:HEAD_DIM], x[HEAD_DIM:2 * HEAD_DIM])

    items = [(slice(c * DN_CHUNK, (c + 1) * DN_CHUNK), slice(gidx * GW, (gidx + 1) * GW), gidx)
             for c in range(nchunk) for gidx in range(NG)]

    local = []

    def gdn_stream():
        for b0 in range(0, len(items), GDN_BATCH):
            batch = items[b0:b0 + GDN_BATCH]
            n = len(batch)
            for rows, _, _ in batch:
                rb0 = rows.start // ROW_BLK * ROW_BLK
                assert ("conv", rb0) in emitted and ("ba", rb0) in emitted
            q2 = [qn_ref[rows, lanes] for rows, lanes, _ in batch]
            k2 = [kn_ref[rows, lanes] for rows, lanes, _ in batch]
            beta2 = [bb_ref[rows, lanes] for rows, lanes, _ in batch]
            gcs = {}
            for rows, _, _ in batch:
                if rows.start not in gcs:
                    gcs[rows.start] = _sel_dot(tri, gb_ref[rows, :])
            gc = [gcs[rows.start][:, lanes] for rows, lanes, _ in batch]
            kb2 = [k2[t] * beta2[t] for t in range(n)]
            kq = [_dot_nt(jnp.concatenate([kb2[t], q2[t]], axis=0), bd(k2[t])) for t in range(n)]
            yield
            grow = [jnp.sum(jnp.where(diag, g, 0.0), axis=0, keepdims=True) for g in gc]
            dec = [jnp.exp(jnp.where(causal, gc[t] - grow[t], -jnp.inf)) for t in range(n)]
            nmat = [jnp.where(strict, -(kq[t][0:DN_CHUNK] * dec[t]), 0.0) for t in range(n)]
            amat = [kq[t][DN_CHUNK:2 * DN_CHUNK] * dec[t] for t in range(n)]
            xinv = [eye2 + m for m in nmat]
            pw = [_dot(m, bd(m)) for m in nmat]
            yield
            for r in range(1, 6):
                if r < 5:
                    res = [_dot(pw[t], jnp.concatenate([bd(pw[t]), bd(xinv[t])], axis=1)) for t in range(n)]
                    pw = [x[:, 0:GW] for x in res]
                    xinv = [xinv[t] + res[t][:, GW:2 * GW] for t in range(n)]
                else:
                    xinv = [xinv[t] + _dot(pw[t], bd(xinv[t])) for t in range(n)]
                yield
            eg = [jnp.exp(g) for g in gc]
            rhs = []
            for t in range(n):
                rows, lanes, _ = batch[t]
                rhs.append(jnp.concatenate([bd(vn_ref[rows, lanes] * beta2[t]), bd(kb2[t] * eg[t])], axis=1))
            uw = [_dot(xinv[t], rhs[t]) for t in range(n)]
            yield
            glast = [g[DN_CHUNK - 1:DN_CHUNK, :] for g in gc]
            kuw = [_dot_tn(k2[t] * jnp.exp(glast[t] - gc[t]), uw[t]) for t in range(n)]
            auw = [_dot(amat[t], jnp.concatenate([bd(uw[t][:, 0:GW]), bd(uw[t][:, GW:2 * GW])], axis=1))
                   for t in range(n)]
            yield
            for t in range(n):
                rows, lanes, gidx = batch[t]
                m2 = eye2 * jnp.exp(glast[t]) - pack_diag(kuw[t][:, GW:2 * GW])
                r2 = pack_diag(kuw[t][:, 0:GW])
                qeff = q2[t] * eg[t] - auw[t][:, GW:2 * GW]
                local.append((rows, lanes, gidx, m2, r2, qeff, auw[t][:, 0:GW]))
            yield

    def chain_stream():
        states = [s_ref[gidx] for gidx in range(NG)]
        outs = []
        for rows, lanes, gidx, m2, r2, qeff, oloc in local:
            res = _dot(jnp.concatenate([qeff, m2], axis=0), bd(states[gidx]))
            outs.append(oloc + res[0:DN_CHUNK])
            states[gidx] = res[DN_CHUNK:2 * DN_CHUNK] + r2
            if gidx == NG - 1:
                od = jnp.concatenate(outs, axis=1)
                outs = []
                mix_ref[rows, SWA_WIDTH:SWA_WIDTH + DN_WIDTH] = (
                    od * lax.rsqrt(_dot(od * od, head_ones) * (1.0 / HEAD_DIM) + EPS) * dn_g_ref[...])
                yield
        for gidx in range(NG):
            s_ref[gidx] = states[gidx]

    def mem_stream():
        for r0 in range(0, ts, MEM_ROWS):
            rows = slice(r0, r0 + MEM_ROWS)
            assert ("mq", r0) in emitted
            for p in range(MEM_HEADS // 2):
                q2 = mq_ref[rows, p * LANES:(p + 1) * LANES]
                acc = None
                for half in range(2):
                    s = _dot_nt(q2, mkv_ref[p * 2 + half])
                    yield
                    e = jnp.exp2(s - jnp.max(s, axis=-1, keepdims=True))
                    o = _dot(e, mkv_ref[4 + p * 2 + half]) * (1.0 / jnp.sum(e, axis=-1, keepdims=True))
                    acc = o if acc is None else acc + o
                    yield
                off = SWA_WIDTH + DN_WIDTH + p * LANES
                mix_ref[rows, off:off + LANES] = acc

    blk_per_rb = ROW_BLK // WINDOW
    swa_streams = [_delayed(SWA_DELAY + rb * PROJ_STEPS_PER_BLK,
                            swa_stream(range(rb * blk_per_rb, (rb + 1) * blk_per_rb)))
                   for rb in range(ts // ROW_BLK)]
    _round_robin([proj_stream(), _delayed(CONV_DELAY, conv_stream())] + swa_streams
                 + [_delayed(GDN_DELAY, gdn_stream()), _delayed(MEM_DELAY, mem_stream())])
    for ref in (kk_ref, vv_ref):
        ref[:, 0:WINDOW, :] = ref[:, ts:ts + WINDOW, :]

    def out_stream(r0):
        rows = slice(r0, r0 + OUT_ROWS)
        mixed = (mix_ref[rows, :] * _silu(gate_ref[rows, :])).astype(BF16)
        yield
        y = jnp.dot(mixed, w_out_ref[...], preferred_element_type=F32)
        yield
        out_ref[0, rows, :] = x_ref[0, rows, :] + _rms(y, post_g_ref[...])

    chunks_per_unit = OUT_ROWS // DN_CHUNK
    _round_robin([chain_stream()] + [_delayed(chunks_per_unit * (k + 1), out_stream(r0))
                                     for k, r0 in enumerate(range(0, ts, OUT_ROWS))])


def _layer(layer, x, mem, sinks, pre_g, w_in, w_ba, conv_w, alog, dtb, dn_g, mem_g, w_mem, w_out, post_g):
    b, s, d = x.shape
    ts = min(SEQ_TILE, s)
    assert s % ts == 0 and ts % ROW_BLK == 0 and d == D_MODEL
    this_layer = lambda bi, si: (layer, 0, 0)
    return pl.pallas_call(
        functools.partial(_layer_kernel, ts=ts, layer=layer),
        out_shape=jax.ShapeDtypeStruct(x.shape, x.dtype),
        grid=(b, s // ts),
        in_specs=[
            pl.BlockSpec(memory_space=pltpu.SMEM),
            pl.BlockSpec((1, ts, d), lambda bi, si: (bi, si, 0)),
            pl.BlockSpec((1, N_MEM, d), lambda bi, si: (bi, 0, 0)),
            pl.BlockSpec((None, 1, d), this_layer),
            pl.BlockSpec((None, d, IN_COLS), this_layer),
            pl.BlockSpec((None, BA_ROWS, d), this_layer),
            pl.BlockSpec((None, DN_CONV, 3 * DN_WIDTH), this_layer),
            pl.BlockSpec((None, BA_ROWS, 1), this_layer),
            pl.BlockSpec((None, BA_ROWS, 1), this_layer),
            pl.BlockSpec((None, 1, DN_WIDTH), this_layer),
            pl.BlockSpec((None, 1, d), this_layer),
            pl.BlockSpec((None, d, 2 * MEM_WIDTH), this_layer),
            pl.BlockSpec((None, MIX_WIDTH, d), this_layer),
            pl.BlockSpec((None, 1, d), this_layer),
        ],
        out_specs=pl.BlockSpec((1, ts, d), lambda bi, si: (bi, si, 0)),
        scratch_shapes=[
            pltpu.VMEM((ts, d), BF16),
            pltpu.VMEM((ts, SWA_WIDTH), BF16),
            pltpu.VMEM((4, WINDOW + ts, LANES), BF16),
            pltpu.VMEM((4, WINDOW + ts, LANES), BF16),
            pltpu.VMEM((CONV_HIST + ts, 3 * DN_WIDTH), F32),
            pltpu.VMEM((ts, DN_WIDTH), F32),
            pltpu.VMEM((ts, DN_WIDTH), F32),
            pltpu.VMEM((ts, DN_WIDTH), F32),
            pltpu.VMEM((ts, DN_WIDTH), F32),
            pltpu.VMEM((ts, DN_WIDTH), F32),
            pltpu.VMEM((NG, DN_CHUNK, GW), F32),
            pltpu.VMEM((8, N_MEM, LANES), BF16),
            pltpu.VMEM((ts, MEM_WIDTH), BF16),
            pltpu.VMEM((ts, MIX_WIDTH), F32),
            pltpu.VMEM((ts, MIX_WIDTH), F32),
        ],
        compiler_params=pltpu.CompilerParams(
            dimension_semantics=("arbitrary", "arbitrary"),
            vmem_limit_bytes=VMEM_LIMIT_BYTES),
        name="hybrid_layer",
    )(sinks, x, mem, pre_g, w_in, w_ba, conv_w, alog, dtb, dn_g, mem_g, w_mem, w_out, post_g)


_BA_COL = SWA_WIDTH + 2 * LANES + 3 * DN_WIDTH
_AFTER_BA = _BA_COL + 2 * DN_HEADS


def _prep_w_in(w_in):
    main = jnp.concatenate([w_in[:, :, 0:_BA_COL], w_in[:, :, _AFTER_BA:]], axis=2)
    col = lax.broadcasted_iota(jnp.int32, (1, 1, IN_COLS), 2)
    is_query = (col < SWA_WIDTH) | ((col >= C_MQ) & (col < C_MQ + MEM_WIDTH))
    main = main * jnp.where(is_query, SCALE * LOG2E, 1.0)
    w_ba = jnp.swapaxes(w_in[:, :, _BA_COL:_AFTER_BA], 1, 2)
    w_ba = jnp.pad(w_ba, ((0, 0), (0, BA_ROWS - 2 * DN_HEADS), (0, 0)))
    return main.astype(BF16), w_ba.astype(BF16)


def _row_place(v, start):
    return jnp.pad(v.astype(F32), ((0, 0), (start, BA_ROWS - start - v.shape[1])))[:, :, None]


@jax.jit
def kernel(x, mem, pre_norm_g, w_in, conv_w, a_log, dt_bias, sinks, dn_norm_g, mem_norm_g, w_mem_kv,
           w_out, post_norm_g):
    w_main, w_ba = _prep_w_in(w_in)
    params = (sinks.astype(F32), pre_norm_g[:, None, :], w_main, w_ba, conv_w,
              _row_place(a_log, DN_HEADS), _row_place(dt_bias, DN_HEADS),
              jnp.tile(dn_norm_g, (1, DN_HEADS))[:, None, :], mem_norm_g[:, None, :],
              w_mem_kv.astype(BF16), w_out.astype(BF16), post_norm_g[:, None, :])
    for layer in range(w_in.shape[0]):
        x = _layer(layer, x, mem, *params)
    return x
```

```python
import functools

import jax
import jax.numpy as jnp
from jax import lax
from jax.experimental import pallas as pl
from jax.experimental.pallas import tpu as pltpu

F32 = jnp.float32
BF16 = jnp.bfloat16

D_MODEL = 1024
HEAD_DIM = 64
SWA_HEADS = 8
SWA_KV_HEADS = 2
SWA_WIDTH = SWA_HEADS * HEAD_DIM
WINDOW = 128
DN_HEADS = 4
DN_WIDTH = DN_HEADS * HEAD_DIM
DN_CONV = 4
DN_CHUNK = 64
N_MEM = 256
MEM_HEADS = 4
MEM_WIDTH = MEM_HEADS * HEAD_DIM
MIX_WIDTH = 1024
EPS = 1e-6
SCALE = HEAD_DIM ** -0.5
LOG2E = 1.4426950408889634

LANES = 128
MXU_N = 256
CONV_HIST = 8

C_SWA = 0
C_DN = 768
C_MQ = 1536
C_GATE = 1792
IN_COLS = 2816
BA_ROWS = 16

GW = 128
NG = DN_WIDTH // GW
HPG = GW // HEAD_DIM

SEQ_TILE = 1024
GDN_BATCH = 8
ROW_BLK = 256
MEM_ROWS = ROW_BLK
OUT_ROWS = 256
PROJ_STEPS_PER_BLK = 8
CONV_DELAY = 3
SWA_DELAY = 6
GDN_DELAY = 7
MEM_DELAY = 10
VMEM_LIMIT_BYTES = 60 * 1024 * 1024


def _dot(a, b):
    return jnp.dot(a.astype(BF16), b.astype(BF16), preferred_element_type=F32)


def _dot_nt(a, b):
    return lax.dot_general(a.astype(BF16), b.astype(BF16), (((1,), (1,)), ((), ())),
                           preferred_element_type=F32)


def _dot_tn(a, b):
    return lax.dot_general(a.astype(BF16), b.astype(BF16), (((0,), (0,)), ((), ())),
                           preferred_element_type=F32)


def _split2(x):
    x0 = x.astype(BF16)
    x1 = (x - x0.astype(F32)).astype(BF16)
    return x0, x1


def _sel_dot(m, x):
    x0, x1 = _split2(x)
    return jnp.dot(m, x0, preferred_element_type=F32) + jnp.dot(m, x1, preferred_element_type=F32)


def _iota(shape, dim):
    return lax.broadcasted_iota(jnp.int32, shape, dim)


def _rms(x, g):
    return x * lax.rsqrt(jnp.mean(x * x, axis=-1, keepdims=True) + EPS) * g


def _silu(x):
    hx = 0.5 * x
    return hx + hx * jnp.tanh(hx)


def _round_robin(streams):
    streams = list(streams)
    while streams:
        for s in list(streams):
            try:
                next(s)
            except StopIteration:
                streams.remove(s)


def _delayed(n, gen):
    for _ in range(n):
        yield
    yield from gen


def _layer_kernel(sinks_ref, x_ref, mem_ref, pre_g_ref, w_in_ref, w_ba_ref, conv_w_ref, alog_ref, dtb_ref,
                  dn_g_ref, mem_g_ref, w_mem_ref, w_out_ref, post_g_ref, out_ref,
                  h_ref, sq_ref, kk_ref, vv_ref, dnp_ref, qn_ref, kn_ref, vn_ref, gb_ref, bb_ref,
                  s_ref, mkv_ref, mq_ref, gate_ref, mix_ref, *, ts, layer):
    i = pl.program_id(1)
    nblk = ts // WINDOW
    nchunk = ts // DN_CHUNK

    lane = _iota((1, LANES), 1)
    lo = lane < HEAD_DIM
    emitted = set()

    @pl.when(i == 0)
    def _():
        s_ref[...] = jnp.zeros_like(s_ref)
        kk_ref[:, 0:WINDOW, :] = jnp.zeros((4, WINDOW, LANES), BF16)
        vv_ref[:, 0:WINDOW, :] = jnp.zeros((4, WINDOW, LANES), BF16)
        dnp_ref[0:CONV_HIST, :] = jnp.zeros((CONV_HIST, 3 * DN_WIDTH), F32)
        hm = _rms(mem_ref[0], mem_g_ref[...])
        mkv = _dot(hm, w_mem_ref[...])
        for kind in range(2):
            for p in range(MEM_HEADS // 2):
                blk = mkv[:, kind * MEM_WIDTH + p * LANES: kind * MEM_WIDTH + (p + 1) * LANES]
                mkv_ref[kind * 4 + p * 2 + 0] = jnp.where(lo, blk, 0.0).astype(BF16)
                mkv_ref[kind * 4 + p * 2 + 1] = jnp.where(lo, 0.0, blk).astype(BF16)

    h_ref[...] = _rms(x_ref[0], pre_g_ref[...]).astype(BF16)

    def proj(c0, width, rows):
        return jnp.dot(h_ref[rows, :], w_in_ref[:, c0:c0 + width], preferred_element_type=F32)

    sel_r = _iota((BA_ROWS, DN_WIDTH), 0)
    sel_c = _iota((BA_ROWS, DN_WIDTH), 1) // HEAD_DIM
    beta_sel = (sel_r == sel_c).astype(BF16)
    decay_sel = (sel_r == DN_HEADS + sel_c).astype(BF16)

    def proj_stream():
        for r0 in range(0, ts, ROW_BLK):
            rows = slice(r0, r0 + ROW_BLK)
            hist_rows = slice(CONV_HIST + r0, CONV_HIST + r0 + ROW_BLK)
            kv_rows = slice(WINDOW + r0, WINDOW + r0 + ROW_BLK)
            for c in range(3 * DN_WIDTH // MXU_N):
                dnp_ref[hist_rows, c * MXU_N:(c + 1) * MXU_N] = proj(C_DN + c * MXU_N, MXU_N, rows)
                emitted.add(("dn", r0, c))
                yield
            for c in range(SWA_WIDTH // MXU_N):
                sq_ref[rows, c * MXU_N:(c + 1) * MXU_N] = proj(C_SWA + c * MXU_N, MXU_N, rows).astype(BF16)
                yield
            kvcur = proj(C_SWA + SWA_WIDTH, 2 * LANES, rows)
            for ref, cur in ((kk_ref, kvcur[:, 0:LANES]), (vv_ref, kvcur[:, LANES:2 * LANES])):
                rolled = pltpu.roll(cur, HEAD_DIM, 1)
                ref[0, kv_rows, :] = jnp.where(lo, cur, 0.0).astype(BF16)
                ref[1, kv_rows, :] = jnp.where(lo, 0.0, rolled).astype(BF16)
                ref[2, kv_rows, :] = jnp.where(lo, rolled, 0.0).astype(BF16)
                ref[3, kv_rows, :] = jnp.where(lo, 0.0, cur).astype(BF16)
            emitted.add(("qkv", r0))
            yield
            ba = _dot_nt(w_ba_ref[...], h_ref[rows, :])
            beta = 1.0 / (1.0 + jnp.exp(-ba))
            z = ba + dtb_ref[...]
            softplus = jnp.maximum(z, 0.0) + jnp.log(1.0 + jnp.exp(-jnp.abs(z)))
            glog = -jnp.exp(alog_ref[...]) * softplus
            bb_ref[rows, :] = _dot_tn(beta, beta_sel)
            g0, g1 = _split2(glog)
            gb_ref[rows, :] = _dot_tn(g0, decay_sel) + _dot_tn(g1, decay_sel)
            emitted.add(("ba", r0))
            yield
            mq_ref[rows, :] = proj(C_MQ, MEM_WIDTH, rows).astype(BF16)
            emitted.add(("mq", r0))
            yield
        for r0 in range(0, ts, ROW_BLK):
            rows = slice(r0, r0 + ROW_BLK)
            for c in range(MIX_WIDTH // MXU_N):
                gate_ref[rows, c * MXU_N:(c + 1) * MXU_N] = proj(C_GATE + c * MXU_N, MXU_N, rows)
                yield

    from_prev = _iota((2 * WINDOW, WINDOW), 1) > (_iota((2 * WINDOW, WINDOW), 0) % WINDOW)
    prev_mask = from_prev.astype(BF16)
    cur_mask = 1.0 - prev_mask
    upper = _iota((2 * WINDOW, 1), 0) < WINDOW
    group = SWA_HEADS // SWA_KV_HEADS

    def swa_stream(blocks):
        for j in blocks:
            assert ("qkv", j * WINDOW // ROW_BLK * ROW_BLK) in emitted
            rows = slice(j * WINDOW, (j + 1) * WINDOW)
            krows = slice(j * WINDOW, (j + 2) * WINDOW)
            for kvh in range(SWA_KV_HEADS):
                p0 = kvh * (group // 2)
                q_st = jnp.concatenate([sq_ref[rows, p0 * LANES:(p0 + 1) * LANES],
                                        sq_ref[rows, (p0 + 1) * LANES:(p0 + 2) * LANES]], axis=0)
                acc = None
                for half in range(2):
                    s = _dot_nt(q_st, kk_ref[2 * kvh + half, krows, :])
                    yield
                    s_prev = s[:, 0:WINDOW]
                    if j == 0:
                        s_prev = jnp.where(i > 0, s_prev, -jnp.inf)
                    s = jnp.where(from_prev, s_prev, s[:, WINDOW:2 * WINDOW])
                    sink = jnp.where(upper, sinks_ref[layer, 2 * p0 + half],
                                     sinks_ref[layer, 2 * p0 + 2 + half]) * LOG2E
                    m = jnp.maximum(jnp.max(s, axis=-1, keepdims=True), sink)
                    e = jnp.exp2(s - m)
                    denom = jnp.sum(e, axis=-1, keepdims=True) + jnp.exp2(sink - m)
                    eb = e.astype(BF16)
                    e2 = jnp.concatenate([eb * prev_mask, eb * cur_mask], axis=1)
                    o = _dot(e2, vv_ref[2 * kvh + half, krows, :]) * (1.0 / denom)
                    acc = o if acc is None else acc + o
                    yield
                mix_ref[rows, p0 * LANES:(p0 + 1) * LANES] = acc[0:WINDOW]
                mix_ref[rows, (p0 + 1) * LANES:(p0 + 2) * LANES] = acc[WINDOW:2 * WINDOW]

    head_ones = ((_iota((DN_WIDTH, DN_WIDTH), 0) // HEAD_DIM)
                 == (_iota((DN_WIDTH, DN_WIDTH), 1) // HEAD_DIM)).astype(BF16)

    def conv_stream():
        for r0 in range(0, ts, ROW_BLK):
            rows = slice(r0, r0 + ROW_BLK)
            for part, dst in enumerate((qn_ref, kn_ref, vn_ref)):
                cols = slice(part * DN_WIDTH, (part + 1) * DN_WIDTH)
                assert ("dn", r0, part) in emitted
                xs = dnp_ref[r0:r0 + CONV_HIST + ROW_BLK, cols]
                acc = xs * conv_w_ref[0:1, cols]
                for j in range(1, DN_CONV):
                    acc = pltpu.roll(acc, 1, 0) + xs * conv_w_ref[j:j + 1, cols]
                conv = _silu(acc[CONV_HIST:CONV_HIST + ROW_BLK, :])
                if part == 0:
                    dst[rows, :] = conv * (lax.rsqrt(_dot(conv * conv, head_ones) + EPS) * SCALE)
                elif part == 1:
                    dst[rows, :] = conv * lax.rsqrt(_dot(conv * conv, head_ones) + EPS)
                else:
                    dst[rows, :] = conv
                yield
            emitted.add(("conv", r0))
            for _ in range(PROJ_STEPS_PER_BLK - 3):
                yield
        dnp_ref[0:CONV_HIST, :] = dnp_ref[ts:ts + CONV_HIST, :]

    cc = _iota((DN_CHUNK, GW), 0)
    cs = _iota((DN_CHUNK, GW), 1) % HEAD_DIM
    causal = cc >= cs
    strict = cc > cs
    diag = cc == cs
    eye2 = diag.astype(F32)
    lo_g = _iota((DN_CHUNK, GW), 1) < HEAD_DIM
    tri = (_iota((DN_CHUNK, DN_CHUNK), 0) >= _iota((DN_CHUNK, DN_CHUNK), 1)).astype(BF16)
    bd_mask = ((_iota((GW, GW), 0) // HEAD_DIM) == (_iota((GW, GW), 1) // HEAD_DIM)).astype(BF16)

    def bd(x):
        xb = x.astype(BF16)
        return jnp.concatenate([xb] * HPG, axis=0) * bd_mask

    def pack_diag(x):
        return jnp.where(lo_g, x[0:HEAD_DIM], x[HEAD_DIM:2 * HEAD_DIM])

    items = [(slice(c * DN_CHUNK, (c + 1) * DN_CHUNK), slice(gidx * GW, (gidx + 1) * GW), gidx)
             for c in range(nchunk) for gidx in range(NG)]

    local = []

    def gdn_stream():
        for b0 in range(0, len(items), GDN_BATCH):
            batch = items[b0:b0 + GDN_BATCH]
            n = len(batch)
            for rows, _, _ in batch:
                rb0 = rows.start // ROW_BLK * ROW_BLK
                assert ("conv", rb0) in emitted and ("ba", rb0) in emitted
            q2 = [qn_ref[rows, lanes] for rows, lanes, _ in batch]
            k2 = [kn_ref[rows, lanes] for rows, lanes, _ in batch]
            beta2 = [bb_ref[rows, lanes] for rows, lanes, _ in batch]
            gcs = {}
            for rows, _, _ in batch:
                if rows.start not in gcs:
                    gcs[rows.start] = _sel_dot(tri, gb_ref[rows, :])
            gc = [gcs[rows.start][:, lanes] for rows, lanes, _ in batch]
            kb2 = [k2[t] * beta2[t] for t in range(n)]
            kq = [_dot_nt(jnp.concatenate([kb2[t], q2[t]], axis=0), bd(k2[t])) for t in range(n)]
            yield
            grow = [jnp.sum(jnp.where(diag, g, 0.0), axis=0, keepdims=True) for g in gc]
            dec = [jnp.exp(jnp.where(causal, gc[t] - grow[t], -jnp.inf)) for t in range(n)]
            nmat = [jnp.where(strict, -(kq[t][0:DN_CHUNK] * dec[t]), 0.0) for t in range(n)]
            amat = [kq[t][DN_CHUNK:2 * DN_CHUNK] * dec[t] for t in range(n)]
            xinv = [eye2 + m for m in nmat]
            pw = [_dot(m, bd(m)) for m in nmat]
            yield
            for r in range(1, 6):
                if r < 5:
                    res = [_dot(pw[t], jnp.concatenate([bd(pw[t]), bd(xinv[t])], axis=1)) for t in range(n)]
                    pw = [x[:, 0:GW] for x in res]
                    xinv = [xinv[t] + res[t][:, GW:2 * GW] for t in range(n)]
                else:
                    xinv = [xinv[t] + _dot(pw[t], bd(xinv[t])) for t in range(n)]
                yield
            eg = [jnp.exp(g) for g in gc]
            rhs = []
            for t in range(n):
                rows, lanes, _ = batch[t]
                rhs.append(jnp.concatenate([bd(vn_ref[rows, lanes] * beta2[t]), bd(kb2[t] * eg[t])], axis=1))
            uw = [_dot(xinv[t], rhs[t]) for t in range(n)]
            yield
            glast = [g[DN_CHUNK - 1:DN_CHUNK, :] for g in gc]
            kuw = [_dot_tn(k2[t] * jnp.exp(glast[t] - gc[t]), uw[t]) for t in range(n)]
            auw = [_dot(amat[t], jnp.concatenate([bd(uw[t][:, 0:GW]), bd(uw[t][:, GW:2 * GW])], axis=1))
                   for t in range(n)]
            yield
            for t in range(n):
                rows, lanes, gidx = batch[t]
                m2 = eye2 * jnp.exp(glast[t]) - pack_diag(kuw[t][:, GW:2 * GW])
                r2 = pack_diag(kuw[t][:, 0:GW])
                qeff = q2[t] * eg[t] - auw[t][:, GW:2 * GW]
                local.append((rows, lanes, gidx, m2, r2, qeff, auw[t][:, 0:GW]))
            yield

    def chain_stream():
        states = [s_ref[gidx] for gidx in range(NG)]
        outs = []
        for rows, lanes, gidx, m2, r2, qeff, oloc in local:
            res = _dot(jnp.concatenate([qeff, m2], axis=0), bd(states[gidx]))
            outs.append(oloc + res[0:DN_CHUNK])
            states[gidx] = res[DN_CHUNK:2 * DN_CHUNK] + r2
            if gidx == NG - 1:
                od = jnp.concatenate(outs, axis=1)
                outs = []
                mix_ref[rows, SWA_WIDTH:SWA_WIDTH + DN_WIDTH] = (
                    od * lax.rsqrt(_dot(od * od, head_ones) * (1.0 / HEAD_DIM) + EPS) * dn_g_ref[...])
                yield
        for gidx in range(NG):
            s_ref[gidx] = states[gidx]

    def mem_stream():
        for r0 in range(0, ts, MEM_ROWS):
            rows = slice(r0, r0 + MEM_ROWS)
            assert ("mq", r0) in emitted
            for p in range(MEM_HEADS // 2):
                q2 = mq_ref[rows, p * LANES:(p + 1) * LANES]
                acc = None
                for half in range(2):
                    s = _dot_nt(q2, mkv_ref[p * 2 + half])
                    yield
                    e = jnp.exp2(s - jnp.max(s, axis=-1, keepdims=True))
                    o = _dot(e, mkv_ref[4 + p * 2 + half]) * (1.0 / jnp.sum(e, axis=-1, keepdims=True))
                    acc = o if acc is None else acc + o
                    yield
                off = SWA_WIDTH + DN_WIDTH + p * LANES
                mix_ref[rows, off:off + LANES] = acc

    blk_per_rb = ROW_BLK // WINDOW
    swa_streams = [_delayed(SWA_DELAY + rb * PROJ_STEPS_PER_BLK,
                            swa_stream(range(rb * blk_per_rb, (rb + 1) * blk_per_rb)))
                   for rb in range(ts // ROW_BLK)]
    _round_robin([proj_stream(), _delayed(CONV_DELAY, conv_stream())] + swa_streams
                 + [_delayed(GDN_DELAY, gdn_stream()), _delayed(MEM_DELAY, mem_stream())])
    for ref in (kk_ref, vv_ref):
        ref[:, 0:WINDOW, :] = ref[:, ts:ts + WINDOW, :]

    def out_stream(r0):
        rows = slice(r0, r0 + OUT_ROWS)
        mixed = (mix_ref[rows, :] * _silu(gate_ref[rows, :])).astype(BF16)
        yield
        y = jnp.dot(mixed, w_out_ref[...], preferred_element_type=F32)
        yield
        out_ref[0, rows, :] = x_ref[0, rows, :] + _rms(y, post_g_ref[...])

    chunks_per_unit = OUT_ROWS // DN_CHUNK
    _round_robin([chain_stream()] + [_delayed(chunks_per_unit * (k + 1), out_stream(r0))
                                     for k, r0 in enumerate(range(0, ts, OUT_ROWS))])


def _layer(layer, x, mem, sinks, pre_g, w_in, w_ba, conv_w, alog, dtb, dn_g, mem_g, w_mem, w_out, post_g):
    b, s, d = x.shape
    ts = min(SEQ_TILE, s)
    assert s % ts == 0 and ts % ROW_BLK == 0 and d == D_MODEL
    this_layer = lambda bi, si: (layer, 0, 0)
    resident = pl.Buffered(1)
    return pl.pallas_call(
        functools.partial(_layer_kernel, ts=ts, layer=layer),
        out_shape=jax.ShapeDtypeStruct(x.shape, x.dtype),
        grid=(b, s // ts),
        in_specs=[
            pl.BlockSpec(memory_space=pltpu.SMEM),
            pl.BlockSpec((1, ts, d), lambda bi, si: (bi, si, 0)),
            pl.BlockSpec((1, N_MEM, d), lambda bi, si: (bi, 0, 0)),
            pl.BlockSpec((None, 1, d), this_layer),
            pl.BlockSpec((None, d, IN_COLS), this_layer, pipeline_mode=resident),
            pl.BlockSpec((None, BA_ROWS, d), this_layer),
            pl.BlockSpec((None, DN_CONV, 3 * DN_WIDTH), this_layer),
            pl.BlockSpec((None, BA_ROWS, 1), this_layer),
            pl.BlockSpec((None, BA_ROWS, 1), this_layer),
            pl.BlockSpec((None, 1, DN_WIDTH), this_layer),
            pl.BlockSpec((None, 1, d), this_layer),
            pl.BlockSpec((None, d, 2 * MEM_WIDTH), this_layer, pipeline_mode=resident),
            pl.BlockSpec((None, MIX_WIDTH, d), this_layer, pipeline_mode=resident),
            pl.BlockSpec((None, 1, d), this_layer),
        ],
        out_specs=pl.BlockSpec((1, ts, d), lambda bi, si: (bi, si, 0)),
        scratch_shapes=[
            pltpu.VMEM((ts, d), BF16),
            pltpu.VMEM((ts, SWA_WIDTH), BF16),
            pltpu.VMEM((4, WINDOW + ts, LANES), BF16),
            pltpu.VMEM((4, WINDOW + ts, LANES), BF16),
            pltpu.VMEM((CONV_HIST + ts, 3 * DN_WIDTH), F32),
            pltpu.VMEM((ts, DN_WIDTH), F32),
            pltpu.VMEM((ts, DN_WIDTH), F32),
            pltpu.VMEM((ts, DN_WIDTH), F32),
            pltpu.VMEM((ts, DN_WIDTH), F32),
            pltpu.VMEM((ts, DN_WIDTH), F32),
            pltpu.VMEM((NG, DN_CHUNK, GW), F32),
            pltpu.VMEM((8, N_MEM, LANES), BF16),
            pltpu.VMEM((ts, MEM_WIDTH), BF16),
            pltpu.VMEM((ts, MIX_WIDTH), F32),
            pltpu.VMEM((ts, MIX_WIDTH), F32),
        ],
        compiler_params=pltpu.CompilerParams(
            dimension_semantics=("arbitrary", "arbitrary"),
            vmem_limit_bytes=VMEM_LIMIT_BYTES),
        name="hybrid_layer",
    )(sinks, x, mem, pre_g, w_in, w_ba, conv_w, alog, dtb, dn_g, mem_g, w_mem, w_out, post_g)


_BA_COL = SWA_WIDTH + 2 * LANES + 3 * DN_WIDTH
_AFTER_BA = _BA_COL + 2 * DN_HEADS


def _prep_w_in(w_in):
    main = jnp.concatenate([w_in[:, :, 0:_BA_COL], w_in[:, :, _AFTER_BA:]], axis=2)
    col = lax.broadcasted_iota(jnp.int32, (1, 1, IN_COLS), 2)
    is_query = (col < SWA_WIDTH) | ((col >= C_MQ) & (col < C_MQ + MEM_WIDTH))
    main = main * jnp.where(is_query, SCALE * LOG2E, 1.0)
    w_ba = jnp.swapaxes(w_in[:, :, _BA_COL:_AFTER_BA], 1, 2)
    w_ba = jnp.pad(w_ba, ((0, 0), (0, BA_ROWS - 2 * DN_HEADS), (0, 0)))
    return main.astype(BF16), w_ba.astype(BF16)


def _row_place(v, start):
    return jnp.pad(v.astype(F32), ((0, 0), (start, BA_ROWS - start - v.shape[1])))[:, :, None]


@jax.jit
def kernel(x, mem, pre_norm_g, w_in, conv_w, a_log, dt_bias, sinks, dn_norm_g, mem_norm_g, w_mem_kv,
           w_out, post_norm_g):
    w_main, w_ba = _prep_w_in(w_in)
    params = (sinks.astype(F32), pre_norm_g[:, None, :], w_main, w_ba, conv_w,
              _row_place(a_log, DN_HEADS), _row_place(dt_bias, DN_HEADS),
              jnp.tile(dn_norm_g, (1, DN_HEADS))[:, None, :], mem_norm_g[:, None, :],
              w_mem_kv.astype(BF16), w_out.astype(BF16), post_norm_g[:, None, :])
    for layer in range(w_in.shape[0]):
        x = _layer(layer, x, mem, *params)
    return x
```

```python
import functools

import jax
import jax.numpy as jnp
from jax import lax
from jax.experimental import pallas as pl
from jax.experimental.pallas import tpu as pltpu

F32 = jnp.float32
BF16 = jnp.bfloat16

D_MODEL = 1024
HEAD_DIM = 64
SWA_HEADS = 8
SWA_KV_HEADS = 2
SWA_WIDTH = SWA_HEADS * HEAD_DIM
WINDOW = 128
DN_HEADS = 4
DN_WIDTH = DN_HEADS * HEAD_DIM
DN_CONV = 4
DN_CHUNK = 64
N_MEM = 256
MEM_HEADS = 4
MEM_WIDTH = MEM_HEADS * HEAD_DIM
MIX_WIDTH = 1024
EPS = 1e-6
SCALE = HEAD_DIM ** -0.5
LOG2E = 1.4426950408889634

LANES = 128
MXU_N = 256
CONV_HIST = 8

C_SWA = 0
C_DN = 768
C_MQ = 1536
C_GATE = 1792
IN_COLS = 2816
BA_ROWS = 16

GW = 128
NG = DN_WIDTH // GW
HPG = GW // HEAD_DIM

SEQ_TILE = 512
GDN_BATCH = 8
ROW_BLK = 256
PROJ_ROWS = 512
MEM_ROWS = ROW_BLK
OUT_ROWS = 256
PROJ_STEPS_PER_BLK = 8
CONV_DELAY = 3
SWA_DELAY = 6
GDN_DELAY = 7
MEM_DELAY = 10
VMEM_LIMIT_BYTES = 60 * 1024 * 1024


def _dot(a, b):
    return jnp.dot(a.astype(BF16), b.astype(BF16), preferred_element_type=F32)


def _dot_nt(a, b):
    return lax.dot_general(a.astype(BF16), b.astype(BF16), (((1,), (1,)), ((), ())),
                           preferred_element_type=F32)


def _dot_tn(a, b):
    return lax.dot_general(a.astype(BF16), b.astype(BF16), (((0,), (0,)), ((), ())),
                           preferred_element_type=F32)


def _split2(x):
    x0 = x.astype(BF16)
    x1 = (x - x0.astype(F32)).astype(BF16)
    return x0, x1


def _sel_dot(m, x):
    x0, x1 = _split2(x)
    return jnp.dot(m, x0, preferred_element_type=F32) + jnp.dot(m, x1, preferred_element_type=F32)


def _iota(shape, dim):
    return lax.broadcasted_iota(jnp.int32, shape, dim)


def _rms(x, g):
    return x * lax.rsqrt(jnp.mean(x * x, axis=-1, keepdims=True) + EPS) * g


def _silu(x):
    hx = 0.5 * x
    return hx + hx * jnp.tanh(hx)


def _round_robin(streams):
    streams = list(streams)
    while streams:
        for s in list(streams):
            try:
                next(s)
            except StopIteration:
                streams.remove(s)


def _delayed(n, gen):
    for _ in range(n):
        yield
    yield from gen


def _layer_kernel(sinks_ref, x_ref, mem_ref, pre_g_ref, w_in_ref, w_ba_ref, conv_w_ref, alog_ref, dtb_ref,
                  dn_g_ref, mem_g_ref, w_mem_ref, w_out_ref, post_g_ref, out_ref,
                  h_ref, sq_ref, kk_ref, vv_ref, dnp_ref, qn_ref, kn_ref, vn_ref, gb_ref, bb_ref,
                  s_ref, mkv_ref, mq_ref, gate_ref, mix_ref, *, ts, layer):
    i = pl.program_id(1)
    nblk = ts // WINDOW
    nchunk = ts // DN_CHUNK

    lane = _iota((1, LANES), 1)
    lo = lane < HEAD_DIM
    emitted = set()

    @pl.when(i == 0)
    def _():
        s_ref[...] = jnp.zeros_like(s_ref)
        kk_ref[:, 0:WINDOW, :] = jnp.zeros((4, WINDOW, LANES), BF16)
        vv_ref[:, 0:WINDOW, :] = jnp.zeros((4, WINDOW, LANES), BF16)
        dnp_ref[0:CONV_HIST, :] = jnp.zeros((CONV_HIST, 3 * DN_WIDTH), F32)
        hm = _rms(mem_ref[0], mem_g_ref[...])
        mkv = _dot(hm, w_mem_ref[...])
        for kind in range(2):
            for p in range(MEM_HEADS // 2):
                blk = mkv[:, kind * MEM_WIDTH + p * LANES: kind * MEM_WIDTH + (p + 1) * LANES]
                mkv_ref[kind * 4 + p * 2 + 0] = jnp.where(lo, blk, 0.0).astype(BF16)
                mkv_ref[kind * 4 + p * 2 + 1] = jnp.where(lo, 0.0, blk).astype(BF16)

    h_ref[...] = _rms(x_ref[0], pre_g_ref[...]).astype(BF16)

    def proj(c0, width, rows):
        return jnp.dot(h_ref[rows, :], w_in_ref[:, c0:c0 + width], preferred_element_type=F32)

    sel_r = _iota((BA_ROWS, DN_WIDTH), 0)
    sel_c = _iota((BA_ROWS, DN_WIDTH), 1) // HEAD_DIM
    beta_sel = (sel_r == sel_c).astype(BF16)
    decay_sel = (sel_r == DN_HEADS + sel_c).astype(BF16)

    def proj_stream():
        for r0 in range(0, ts, PROJ_ROWS):
            rows = slice(r0, r0 + PROJ_ROWS)
            hist_rows = slice(CONV_HIST + r0, CONV_HIST + r0 + PROJ_ROWS)
            kv_rows = slice(WINDOW + r0, WINDOW + r0 + PROJ_ROWS)
            blocks = range(r0, r0 + PROJ_ROWS, ROW_BLK)
            for c in range(3 * DN_WIDTH // MXU_N):
                dnp_ref[hist_rows, c * MXU_N:(c + 1) * MXU_N] = proj(C_DN + c * MXU_N, MXU_N, rows)
                emitted.update(("dn", rb, c) for rb in blocks)
                yield
            for c in range(SWA_WIDTH // MXU_N):
                sq_ref[rows, c * MXU_N:(c + 1) * MXU_N] = proj(C_SWA + c * MXU_N, MXU_N, rows).astype(BF16)
                yield
            kvcur = proj(C_SWA + SWA_WIDTH, 2 * LANES, rows)
            for ref, cur in ((kk_ref, kvcur[:, 0:LANES]), (vv_ref, kvcur[:, LANES:2 * LANES])):
                rolled = pltpu.roll(cur, HEAD_DIM, 1)
                ref[0, kv_rows, :] = jnp.where(lo, cur, 0.0).astype(BF16)
                ref[1, kv_rows, :] = jnp.where(lo, 0.0, rolled).astype(BF16)
                ref[2, kv_rows, :] = jnp.where(lo, rolled, 0.0).astype(BF16)
                ref[3, kv_rows, :] = jnp.where(lo, 0.0, cur).astype(BF16)
            emitted.update(("qkv", rb) for rb in blocks)
            yield
            ba = _dot_nt(w_ba_ref[...], h_ref[rows, :])
            beta = 1.0 / (1.0 + jnp.exp(-ba))
            z = ba + dtb_ref[...]
            softplus = jnp.maximum(z, 0.0) + jnp.log(1.0 + jnp.exp(-jnp.abs(z)))
            glog = -jnp.exp(alog_ref[...]) * softplus
            bb_ref[rows, :] = _dot_tn(beta, beta_sel)
            g0, g1 = _split2(glog)
            gb_ref[rows, :] = _dot_tn(g0, decay_sel) + _dot_tn(g1, decay_sel)
            emitted.update(("ba", rb) for rb in blocks)
            yield
            mq_ref[rows, :] = proj(C_MQ, MEM_WIDTH, rows).astype(BF16)
            emitted.update(("mq", rb) for rb in blocks)
            yield
        for r0 in range(0, ts, PROJ_ROWS):
            rows = slice(r0, r0 + PROJ_ROWS)
            for c in range(MIX_WIDTH // MXU_N):
                gate_ref[rows, c * MXU_N:(c + 1) * MXU_N] = proj(C_GATE + c * MXU_N, MXU_N, rows)
                yield

    from_prev = _iota((2 * WINDOW, WINDOW), 1) > (_iota((2 * WINDOW, WINDOW), 0) % WINDOW)
    prev_mask = from_prev.astype(BF16)
    cur_mask = 1.0 - prev_mask
    upper = _iota((2 * WINDOW, 1), 0) < WINDOW
    group = SWA_HEADS // SWA_KV_HEADS

    def swa_stream(blocks):
        for j in blocks:
            assert ("qkv", j * WINDOW // ROW_BLK * ROW_BLK) in emitted
            rows = slice(j * WINDOW, (j + 1) * WINDOW)
            krows = slice(j * WINDOW, (j + 2) * WINDOW)
            for kvh in range(SWA_KV_HEADS):
                p0 = kvh * (group // 2)
                q_st = jnp.concatenate([sq_ref[rows, p0 * LANES:(p0 + 1) * LANES],
                                        sq_ref[rows, (p0 + 1) * LANES:(p0 + 2) * LANES]], axis=0)
                acc = None
                for half in range(2):
                    s = _dot_nt(q_st, kk_ref[2 * kvh + half, krows, :])
                    yield
                    s_prev = s[:, 0:WINDOW]
                    if j == 0:
                        s_prev = jnp.where(i > 0, s_prev, -jnp.inf)
                    s = jnp.where(from_prev, s_prev, s[:, WINDOW:2 * WINDOW])
                    sink = jnp.where(upper, sinks_ref[layer, 2 * p0 + half],
                                     sinks_ref[layer, 2 * p0 + 2 + half]) * LOG2E
                    m = jnp.maximum(jnp.max(s, axis=-1, keepdims=True), sink)
                    e = jnp.exp2(s - m)
                    denom = jnp.sum(e, axis=-1, keepdims=True) + jnp.exp2(sink - m)
                    eb = e.astype(BF16)
                    e2 = jnp.concatenate([eb * prev_mask, eb * cur_mask], axis=1)
                    o = _dot(e2, vv_ref[2 * kvh + half, krows, :]) * (1.0 / denom)
                    acc = o if acc is None else acc + o
                    yield
                mix_ref[rows, p0 * LANES:(p0 + 1) * LANES] = acc[0:WINDOW]
                mix_ref[rows, (p0 + 1) * LANES:(p0 + 2) * LANES] = acc[WINDOW:2 * WINDOW]

    head_ones = ((_iota((DN_WIDTH, DN_WIDTH), 0) // HEAD_DIM)
                 == (_iota((DN_WIDTH, DN_WIDTH), 1) // HEAD_DIM)).astype(BF16)

    def conv_stream():
        for r0 in range(0, ts, ROW_BLK):
            rows = slice(r0, r0 + ROW_BLK)
            for part, dst in enumerate((qn_ref, kn_ref, vn_ref)):
                cols = slice(part * DN_WIDTH, (part + 1) * DN_WIDTH)
                assert ("dn", r0, part) in emitted
                xs = dnp_ref[r0:r0 + CONV_HIST + ROW_BLK, cols]
                acc = xs * conv_w_ref[0:1, cols]
                for j in range(1, DN_CONV):
                    acc = pltpu.roll(acc, 1, 0) + xs * conv_w_ref[j:j + 1, cols]
                conv = _silu(acc[CONV_HIST:CONV_HIST + ROW_BLK, :])
                if part == 0:
                    dst[rows, :] = conv * (lax.rsqrt(_dot(conv * conv, head_ones) + EPS) * SCALE)
                elif part == 1:
                    dst[rows, :] = conv * lax.rsqrt(_dot(conv * conv, head_ones) + EPS)
                else:
                    dst[rows, :] = conv
                yield
            emitted.add(("conv", r0))
            for _ in range(PROJ_STEPS_PER_BLK - 3):
                yield
        dnp_ref[0:CONV_HIST, :] = dnp_ref[ts:ts + CONV_HIST, :]

    cc = _iota((DN_CHUNK, GW), 0)
    cs = _iota((DN_CHUNK, GW), 1) % HEAD_DIM
    causal = cc >= cs
    strict = cc > cs
    diag = cc == cs
    eye2 = diag.astype(F32)
    lo_g = _iota((DN_CHUNK, GW), 1) < HEAD_DIM
    tri = (_iota((DN_CHUNK, DN_CHUNK), 0) >= _iota((DN_CHUNK, DN_CHUNK), 1)).astype(BF16)
    bd_mask = ((_iota((GW, GW), 0) // HEAD_DIM) == (_iota((GW, GW), 1) // HEAD_DIM)).astype(BF16)

    def bd(x):
        xb = x.astype(BF16)
        return jnp.concatenate([xb] * HPG, axis=0) * bd_mask

    def pack_diag(x):
        return jnp.where(lo_g, x[0:HEAD_DIM], x[HEAD_DIM:2 * HEAD_DIM])

    items = [(slice(c * DN_CHUNK, (c + 1) * DN_CHUNK), slice(gidx * GW, (gidx + 1) * GW), gidx)
             for c in range(nchunk) for gidx in range(NG)]

    local = []

    def gdn_stream():
        for b0 in range(0, len(items), GDN_BATCH):
            batch = items[b0:b0 + GDN_BATCH]
            n = len(batch)
            for rows, _, _ in batch:
                rb0 = rows.start // ROW_BLK * ROW_BLK
                assert ("conv", rb0) in emitted and ("ba", rb0) in emitted
            q2 = [qn_ref[rows, lanes] for rows, lanes, _ in batch]
            k2 = [kn_ref[rows, lanes] for rows, lanes, _ in batch]
            beta2 = [bb_ref[rows, lanes] for rows, lanes, _ in batch]
            gcs = {}
            for rows, _, _ in batch:
                if rows.start not in gcs:
                    gcs[rows.start] = _sel_dot(tri, gb_ref[rows, :])
            gc = [gcs[rows.start][:, lanes] for rows, lanes, _ in batch]
            kb2 = [k2[t] * beta2[t] for t in range(n)]
            kq = [_dot_nt(jnp.concatenate([kb2[t], q2[t]], axis=0), bd(k2[t])) for t in range(n)]
            yield
            grow = [jnp.sum(jnp.where(diag, g, 0.0), axis=0, keepdims=True) for g in gc]
            dec = [jnp.exp(jnp.where(causal, gc[t] - grow[t], -jnp.inf)) for t in range(n)]
            nmat = [jnp.where(strict, -(kq[t][0:DN_CHUNK] * dec[t]), 0.0) for t in range(n)]
            amat = [kq[t][DN_CHUNK:2 * DN_CHUNK] * dec[t] for t in range(n)]
            xinv = [eye2 + m for m in nmat]
            pw = [_dot(m, bd(m)) for m in nmat]
            yield
            for r in range(1, 6):
                if r < 5:
                    res = [_dot(pw[t], jnp.concatenate([bd(pw[t]), bd(xinv[t])], axis=1)) for t in range(n)]
                    pw = [x[:, 0:GW] for x in res]
                    xinv = [xinv[t] + res[t][:, GW:2 * GW] for t in range(n)]
                else:
                    xinv = [xinv[t] + _dot(pw[t], bd(xinv[t])) for t in range(n)]
                yield
            eg = [jnp.exp(g) for g in gc]
            rhs = []
            for t in range(n):
                rows, lanes, _ = batch[t]
                rhs.append(jnp.concatenate([bd(vn_ref[rows, lanes] * beta2[t]), bd(kb2[t] * eg[t])], axis=1))
            uw = [_dot(xinv[t], rhs[t]) for t in range(n)]
            yield
            glast = [g[DN_CHUNK - 1:DN_CHUNK, :] for g in gc]
            kuw = [_dot_tn(k2[t] * jnp.exp(glast[t] - gc[t]), uw[t]) for t in range(n)]
            auw = [_dot(amat[t], jnp.concatenate([bd(uw[t][:, 0:GW]), bd(uw[t][:, GW:2 * GW])], axis=1))
                   for t in range(n)]
            yield
            for t in range(n):
                rows, lanes, gidx = batch[t]
                m2 = eye2 * jnp.exp(glast[t]) - pack_diag(kuw[t][:, GW:2 * GW])
                r2 = pack_diag(kuw[t][:, 0:GW])
                qeff = q2[t] * eg[t] - auw[t][:, GW:2 * GW]
                local.append((rows, lanes, gidx, m2, r2, qeff, auw[t][:, 0:GW]))
            yield

    def chain_stream():
        states = [s_ref[gidx] for gidx in range(NG)]
        outs = []
        for rows, lanes, gidx, m2, r2, qeff, oloc in local:
            res = _dot(jnp.concatenate([qeff, m2], axis=0), bd(states[gidx]))
            outs.append(oloc + res[0:DN_CHUNK])
            states[gidx] = res[DN_CHUNK:2 * DN_CHUNK] + r2
            if gidx == NG - 1:
                od = jnp.concatenate(outs, axis=1)
                outs = []
                mix_ref[rows, SWA_WIDTH:SWA_WIDTH + DN_WIDTH] = (
                    od * lax.rsqrt(_dot(od * od, head_ones) * (1.0 / HEAD_DIM) + EPS) * dn_g_ref[...])
                yield
        for gidx in range(NG):
            s_ref[gidx] = states[gidx]

    def mem_stream():
        for r0 in range(0, ts, MEM_ROWS):
            rows = slice(r0, r0 + MEM_ROWS)
            assert ("mq", r0) in emitted
            for p in range(MEM_HEADS // 2):
                q2 = mq_ref[rows, p * LANES:(p + 1) * LANES]
                acc = None
                for half in range(2):
                    s = _dot_nt(q2, mkv_ref[p * 2 + half])
                    yield
                    e = jnp.exp2(s - jnp.max(s, axis=-1, keepdims=True))
                    o = _dot(e, mkv_ref[4 + p * 2 + half]) * (1.0 / jnp.sum(e, axis=-1, keepdims=True))
                    acc = o if acc is None else acc + o
                    yield
                off = SWA_WIDTH + DN_WIDTH + p * LANES
                mix_ref[rows, off:off + LANES] = acc

    blk_per_rb = ROW_BLK // WINDOW
    swa_streams = [_delayed(SWA_DELAY + rb * PROJ_STEPS_PER_BLK,
                            swa_stream(range(rb * blk_per_rb, (rb + 1) * blk_per_rb)))
                   for rb in range(ts // ROW_BLK)]
    _round_robin([proj_stream(), _delayed(CONV_DELAY, conv_stream())] + swa_streams
                 + [_delayed(GDN_DELAY, gdn_stream()), _delayed(MEM_DELAY, mem_stream())])
    for ref in (kk_ref, vv_ref):
        ref[:, 0:WINDOW, :] = ref[:, ts:ts + WINDOW, :]

    def out_stream(r0):
        rows = slice(r0, r0 + OUT_ROWS)
        mixed = (mix_ref[rows, :] * _silu(gate_ref[rows, :])).astype(BF16)
        yield
        y = jnp.dot(mixed, w_out_ref[...], preferred_element_type=F32)
        yield
        out_ref[0, rows, :] = x_ref[0, rows, :] + _rms(y, post_g_ref[...])

    chunks_per_unit = OUT_ROWS // DN_CHUNK
    _round_robin([chain_stream()] + [_delayed(chunks_per_unit * (k + 1), out_stream(r0))
                                     for k, r0 in enumerate(range(0, ts, OUT_ROWS))])


def _layer(layer, x, mem, sinks, pre_g, w_in, w_ba, conv_w, alog, dtb, dn_g, mem_g, w_mem, w_out, post_g):
    b, s, d = x.shape
    ts = min(SEQ_TILE, s)
    assert s % ts == 0 and ts % PROJ_ROWS == 0 and PROJ_ROWS % ROW_BLK == 0 and d == D_MODEL
    this_layer = lambda bi, si: (layer, 0, 0)
    resident = pl.Buffered(1)
    return pl.pallas_call(
        functools.partial(_layer_kernel, ts=ts, layer=layer),
        out_shape=jax.ShapeDtypeStruct(x.shape, x.dtype),
        grid=(b, s // ts),
        in_specs=[
            pl.BlockSpec(memory_space=pltpu.SMEM),
            pl.BlockSpec((1, ts, d), lambda bi, si: (bi, si, 0)),
            pl.BlockSpec((1, N_MEM, d), lambda bi, si: (bi, 0, 0)),
            pl.BlockSpec((None, 1, d), this_layer),
            pl.BlockSpec((None, d, IN_COLS), this_layer, pipeline_mode=resident),
            pl.BlockSpec((None, BA_ROWS, d), this_layer),
            pl.BlockSpec((None, DN_CONV, 3 * DN_WIDTH), this_layer),
            pl.BlockSpec((None, BA_ROWS, 1), this_layer),
            pl.BlockSpec((None, BA_ROWS, 1), this_layer),
            pl.BlockSpec((None, 1, DN_WIDTH), this_layer),
            pl.BlockSpec((None, 1, d), this_layer),
            pl.BlockSpec((None, d, 2 * MEM_WIDTH), this_layer, pipeline_mode=resident),
            pl.BlockSpec((None, MIX_WIDTH, d), this_layer, pipeline_mode=resident),
            pl.BlockSpec((None, 1, d), this_layer),
        ],
        out_specs=pl.BlockSpec((1, ts, d), lambda bi, si: (bi, si, 0)),
        scratch_shapes=[
            pltpu.VMEM((ts, d), BF16),
            pltpu.VMEM((ts, SWA_WIDTH), BF16),
            pltpu.VMEM((4, WINDOW + ts, LANES), BF16),
            pltpu.VMEM((4, WINDOW + ts, LANES), BF16),
            pltpu.VMEM((CONV_HIST + ts, 3 * DN_WIDTH), F32),
            pltpu.VMEM((ts, DN_WIDTH), F32),
            pltpu.VMEM((ts, DN_WIDTH), F32),
            pltpu.VMEM((ts, DN_WIDTH), F32),
            pltpu.VMEM((ts, DN_WIDTH), F32),
            pltpu.VMEM((ts, DN_WIDTH), F32),
            pltpu.VMEM((NG, DN_CHUNK, GW), F32),
            pltpu.VMEM((8, N_MEM, LANES), BF16),
            pltpu.VMEM((ts, MEM_WIDTH), BF16),
            pltpu.VMEM((ts, MIX_WIDTH), F32),
            pltpu.VMEM((ts, MIX_WIDTH), F32),
        ],
        compiler_params=pltpu.CompilerParams(
            dimension_semantics=("arbitrary", "arbitrary"),
            vmem_limit_bytes=VMEM_LIMIT_BYTES),
        name="hybrid_layer",
    )(sinks, x, mem, pre_g, w_in, w_ba, conv_w, alog, dtb, dn_g, mem_g, w_mem, w_out, post_g)


_BA_COL = SWA_WIDTH + 2 * LANES + 3 * DN_WIDTH
_AFTER_BA = _BA_COL + 2 * DN_HEADS


def _prep_w_in(w_in):
    main = jnp.concatenate([w_in[:, :, 0:_BA_COL], w_in[:, :, _AFTER_BA:]], axis=2)
    col = lax.broadcasted_iota(jnp.int32, (1, 1, IN_COLS), 2)
    is_query = (col < SWA_WIDTH) | ((col >= C_MQ) & (col < C_MQ + MEM_WIDTH))
    main = main * jnp.where(is_query, SCALE * LOG2E, 1.0)
    w_ba = jnp.swapaxes(w_in[:, :, _BA_COL:_AFTER_BA], 1, 2)
    w_ba = jnp.pad(w_ba, ((0, 0), (0, BA_ROWS - 2 * DN_HEADS), (0, 0)))
    return main.astype(BF16), w_ba.astype(BF16)


def _row_place(v, start):
    return jnp.pad(v.astype(F32), ((0, 0), (start, BA_ROWS - start - v.shape[1])))[:, :, None]


@jax.jit
def kernel(x, mem, pre_norm_g, w_in, conv_w, a_log, dt_bias, sinks, dn_norm_g, mem_norm_g, w_mem_kv,
           w_out, post_norm_g):
    w_main, w_ba = _prep_w_in(w_in)
    params = (sinks.astype(F32), pre_norm_g[:, None, :], w_main, w_ba, conv_w,
              _row_place(a_log, DN_HEADS), _row_place(dt_bias, DN_HEADS),
              jnp.tile(dn_norm_g, (1, DN_HEADS))[:, None, :], mem_norm_g[:, None, :],
              w_mem_kv.astype(BF16), w_out.astype(BF16), post_norm_g[:, None, :])
    for layer in range(w_in.shape[0]):
        x = _layer(layer, x, mem, *params)
    return x
```

```python
import functools

import jax
import jax.numpy as jnp
from jax import lax
from jax.experimental import pallas as pl
from jax.experimental.pallas import tpu as pltpu

F32 = jnp.float32
BF16 = jnp.bfloat16

D_MODEL = 1024
HEAD_DIM = 64
SWA_HEADS = 8
SWA_KV_HEADS = 2
SWA_WIDTH = SWA_HEADS * HEAD_DIM
WINDOW = 128
DN_HEADS = 4
DN_WIDTH = DN_HEADS * HEAD_DIM
DN_CONV = 4
DN_CHUNK = 64
N_MEM = 256
MEM_HEADS = 4
MEM_WIDTH = MEM_HEADS * HEAD_DIM
MIX_WIDTH = 1024
EPS = 1e-6
SCALE = HEAD_DIM ** -0.5
LOG2E = 1.4426950408889634

LANES = 128
MXU_N = 256
CONV_HIST = 8

C_SWA = 0
C_DN = 768
C_MQ = 1536
C_GATE = 1792
IN_COLS = 2816
BA_ROWS = 16

GW = 128
NG = DN_WIDTH // GW
HPG = GW // HEAD_DIM

SEQ_TILE = 1024
GDN_BATCH = 8
ROW_BLK = 256
PROJ_ROWS = 256
MEM_ROWS = ROW_BLK
OUT_ROWS = 256
PROJ_STEPS_PER_BLK = 8
CONV_DELAY = 3
SWA_DELAY = 6
GDN_DELAY = 7
MEM_DELAY = 10
VMEM_LIMIT_BYTES = 60 * 1024 * 1024


def _dot(a, b):
    return jnp.dot(a.astype(BF16), b.astype(BF16), preferred_element_type=F32)


def _dot_nt(a, b):
    return lax.dot_general(a.astype(BF16), b.astype(BF16), (((1,), (1,)), ((), ())),
                           preferred_element_type=F32)


def _dot_tn(a, b):
    return lax.dot_general(a.astype(BF16), b.astype(BF16), (((0,), (0,)), ((), ())),
                           preferred_element_type=F32)


def _split2(x):
    x0 = x.astype(BF16)
    x1 = (x - x0.astype(F32)).astype(BF16)
    return x0, x1


def _sel_dot(m, x):
    x0, x1 = _split2(x)
    return jnp.dot(m, x0, preferred_element_type=F32) + jnp.dot(m, x1, preferred_element_type=F32)


def _iota(shape, dim):
    return lax.broadcasted_iota(jnp.int32, shape, dim)


def _rms(x, g):
    return x * lax.rsqrt(jnp.mean(x * x, axis=-1, keepdims=True) + EPS) * g


def _silu(x):
    hx = 0.5 * x
    return hx + hx * jnp.tanh(hx)


def _round_robin(streams):
    streams = list(streams)
    while streams:
        for s in list(streams):
            try:
                next(s)
            except StopIteration:
                streams.remove(s)


def _delayed(n, gen):
    for _ in range(n):
        yield
    yield from gen


def _layer_kernel(sinks_ref, x_ref, mem_ref, pre_g_ref, w_in_ref, w_ba_ref, conv_w_ref, alog_ref, dtb_ref,
                  dn_g_ref, mem_g_ref, w_mem_ref, w_out_ref, post_g_ref, out_ref,
                  h_ref, sq_ref, kk_ref, vv_ref, dnp_ref, qn_ref, kn_ref, vn_ref, gb_ref, bb_ref,
                  s_ref, mkv_ref, mq_ref, gate_ref, mix_ref, *, ts, layer):
    i = pl.program_id(1)
    nblk = ts // WINDOW
    nchunk = ts // DN_CHUNK

    lane = _iota((1, LANES), 1)
    lo = lane < HEAD_DIM
    emitted = set()

    @pl.when(i == 0)
    def _():
        s_ref[...] = jnp.zeros_like(s_ref)
        kk_ref[:, 0:WINDOW, :] = jnp.zeros((4, WINDOW, LANES), BF16)
        vv_ref[:, 0:WINDOW, :] = jnp.zeros((4, WINDOW, LANES), BF16)
        dnp_ref[0:CONV_HIST, :] = jnp.zeros((CONV_HIST, 3 * DN_WIDTH), F32)
        hm = _rms(mem_ref[0], mem_g_ref[...])
        mkv = _dot(hm, w_mem_ref[...])
        for kind in range(2):
            for p in range(MEM_HEADS // 2):
                blk = mkv[:, kind * MEM_WIDTH + p * LANES: kind * MEM_WIDTH + (p + 1) * LANES]
                mkv_ref[kind * 4 + p * 2 + 0] = jnp.where(lo, blk, 0.0).astype(BF16)
                mkv_ref[kind * 4 + p * 2 + 1] = jnp.where(lo, 0.0, blk).astype(BF16)

    h_ref[...] = _rms(x_ref[0], pre_g_ref[...]).astype(BF16)

    def proj(c0, width, rows):
        return jnp.dot(h_ref[rows, :], w_in_ref[:, c0:c0 + width], preferred_element_type=F32)

    sel_r = _iota((BA_ROWS, DN_WIDTH), 0)
    sel_c = _iota((BA_ROWS, DN_WIDTH), 1) // HEAD_DIM
    beta_sel = (sel_r == sel_c).astype(BF16)
    decay_sel = (sel_r == DN_HEADS + sel_c).astype(BF16)

    def proj_stream():
        for r0 in range(0, ts, PROJ_ROWS):
            rows = slice(r0, r0 + PROJ_ROWS)
            hist_rows = slice(CONV_HIST + r0, CONV_HIST + r0 + PROJ_ROWS)
            kv_rows = slice(WINDOW + r0, WINDOW + r0 + PROJ_ROWS)
            blocks = range(r0, r0 + PROJ_ROWS, ROW_BLK)
            for c in range(3 * DN_WIDTH // MXU_N):
                dnp_ref[hist_rows, c * MXU_N:(c + 1) * MXU_N] = proj(C_DN + c * MXU_N, MXU_N, rows)
                emitted.update(("dn", rb, c) for rb in blocks)
                yield
            for c in range(SWA_WIDTH // MXU_N):
                sq_ref[rows, c * MXU_N:(c + 1) * MXU_N] = proj(C_SWA + c * MXU_N, MXU_N, rows).astype(BF16)
                yield
            kvcur = proj(C_SWA + SWA_WIDTH, 2 * LANES, rows)
            for ref, cur in ((kk_ref, kvcur[:, 0:LANES]), (vv_ref, kvcur[:, LANES:2 * LANES])):
                rolled = pltpu.roll(cur, HEAD_DIM, 1)
                ref[0, kv_rows, :] = jnp.where(lo, cur, 0.0).astype(BF16)
                ref[1, kv_rows, :] = jnp.where(lo, 0.0, rolled).astype(BF16)
                ref[2, kv_rows, :] = jnp.where(lo, rolled, 0.0).astype(BF16)
                ref[3, kv_rows, :] = jnp.where(lo, 0.0, cur).astype(BF16)
            emitted.update(("qkv", rb) for rb in blocks)
            yield
            ba = _dot_nt(w_ba_ref[...], h_ref[rows, :])
            beta = 1.0 / (1.0 + jnp.exp(-ba))
            z = ba + dtb_ref[...]
            softplus = jnp.maximum(z, 0.0) + jnp.log(1.0 + jnp.exp(-jnp.abs(z)))
            glog = -jnp.exp(alog_ref[...]) * softplus
            bb_ref[rows, :] = _dot_tn(beta, beta_sel)
            g0, g1 = _split2(glog)
            gb_ref[rows, :] = _dot_tn(g0, decay_sel) + _dot_tn(g1, decay_sel)
            emitted.update(("ba", rb) for rb in blocks)
            yield
            mq_ref[rows, :] = proj(C_MQ, MEM_WIDTH, rows).astype(BF16)
            emitted.update(("mq", rb) for rb in blocks)
            yield
        for r0 in range(0, ts, PROJ_ROWS):
            rows = slice(r0, r0 + PROJ_ROWS)
            for c in range(MIX_WIDTH // MXU_N):
                gate_ref[rows, c * MXU_N:(c + 1) * MXU_N] = proj(C_GATE + c * MXU_N, MXU_N, rows)
                yield

    from_prev = _iota((2 * WINDOW, WINDOW), 1) > (_iota((2 * WINDOW, WINDOW), 0) % WINDOW)
    prev_mask = from_prev.astype(BF16)
    cur_mask = 1.0 - prev_mask
    upper = _iota((2 * WINDOW, 1), 0) < WINDOW
    group = SWA_HEADS // SWA_KV_HEADS

    def swa_stream(blocks):
        for j in blocks:
            assert ("qkv", j * WINDOW // ROW_BLK * ROW_BLK) in emitted
            rows = slice(j * WINDOW, (j + 1) * WINDOW)
            krows = slice(j * WINDOW, (j + 2) * WINDOW)
            for kvh in range(SWA_KV_HEADS):
                p0 = kvh * (group // 2)
                q_st = jnp.concatenate([sq_ref[rows, p0 * LANES:(p0 + 1) * LANES],
                                        sq_ref[rows, (p0 + 1) * LANES:(p0 + 2) * LANES]], axis=0)
                acc = None
                for half in range(2):
                    s = _dot_nt(q_st, kk_ref[2 * kvh + half, krows, :])
                    yield
                    s_prev = s[:, 0:WINDOW]
                    if j == 0:
                        s_prev = jnp.where(i > 0, s_prev, -jnp.inf)
                    s = jnp.where(from_prev, s_prev, s[:, WINDOW:2 * WINDOW])
                    sink = jnp.where(upper, sinks_ref[layer, 2 * p0 + half],
                                     sinks_ref[layer, 2 * p0 + 2 + half]) * LOG2E
                    m = jnp.maximum(jnp.max(s, axis=-1, keepdims=True), sink)
                    e = jnp.exp2(s - m)
                    denom = jnp.sum(e, axis=-1, keepdims=True) + jnp.exp2(sink - m)
                    eb = e.astype(BF16)
                    e2 = jnp.concatenate([eb * prev_mask, eb * cur_mask], axis=1)
                    o = _dot(e2, vv_ref[2 * kvh + half, krows, :]) * (1.0 / denom)
                    acc = o if acc is None else acc + o
                    yield
                mix_ref[rows, p0 * LANES:(p0 + 1) * LANES] = acc[0:WINDOW]
                mix_ref[rows, (p0 + 1) * LANES:(p0 + 2) * LANES] = acc[WINDOW:2 * WINDOW]

    head_ones = ((_iota((DN_WIDTH, DN_WIDTH), 0) // HEAD_DIM)
                 == (_iota((DN_WIDTH, DN_WIDTH), 1) // HEAD_DIM)).astype(BF16)

    def conv_stream():
        for r0 in range(0, ts, ROW_BLK):
            rows = slice(r0, r0 + ROW_BLK)
            for part, dst in enumerate((qn_ref, kn_ref, vn_ref)):
                cols = slice(part * DN_WIDTH, (part + 1) * DN_WIDTH)
                assert ("dn", r0, part) in emitted
                xs = dnp_ref[r0:r0 + CONV_HIST + ROW_BLK, cols]
                acc = xs * conv_w_ref[0:1, cols]
                for j in range(1, DN_CONV):
                    acc = pltpu.roll(acc, 1, 0) + xs * conv_w_ref[j:j + 1, cols]
                conv = _silu(acc[CONV_HIST:CONV_HIST + ROW_BLK, :])
                if part == 0:
                    dst[rows, :] = conv * (lax.rsqrt(_dot(conv * conv, head_ones) + EPS) * SCALE)
                elif part == 1:
                    dst[rows, :] = conv * lax.rsqrt(_dot(conv * conv, head_ones) + EPS)
                else:
                    dst[rows, :] = conv
                yield
            emitted.add(("conv", r0))
            for _ in range(PROJ_STEPS_PER_BLK - 3):
                yield
        dnp_ref[0:CONV_HIST, :] = dnp_ref[ts:ts + CONV_HIST, :]

    cc = _iota((DN_CHUNK, GW), 0)
    cs = _iota((DN_CHUNK, GW), 1) % HEAD_DIM
    causal = cc >= cs
    strict = cc > cs
    diag = cc == cs
    eye2 = diag.astype(F32)
    lo_g = _iota((DN_CHUNK, GW), 1) < HEAD_DIM
    tri = (_iota((DN_CHUNK, DN_CHUNK), 0) >= _iota((DN_CHUNK, DN_CHUNK), 1)).astype(BF16)
    bd_mask = ((_iota((GW, GW), 0) // HEAD_DIM) == (_iota((GW, GW), 1) // HEAD_DIM)).astype(BF16)

    def bd(x):
        xb = x.astype(BF16)
        return jnp.concatenate([xb] * HPG, axis=0) * bd_mask

    half = DN_CHUNK // 2
    sub_lo = (_iota((half, GW), 1) % HEAD_DIM) < half
    eye4 = (_iota((half, GW), 0) == _iota((half, GW), 1) % half).astype(F32)
    blk_r = _iota((GW, GW), 0) // half
    blk_c = _iota((GW, GW), 1) // half
    diag4_mask = (blk_r == blk_c).astype(BF16)
    low_mask = ((blk_r == blk_c + 1) & (blk_c % 2 == 0)).astype(BF16)
    first_mask = ((blk_r == blk_c) & (blk_c % 2 == 0)).astype(BF16)

    def tile4(x, mask):
        xb = x.astype(BF16)
        return jnp.concatenate([xb] * (GW // half), axis=0) * mask

    def pack_diag(x):
        return jnp.where(lo_g, x[0:HEAD_DIM], x[HEAD_DIM:2 * HEAD_DIM])

    items = [(slice(c * DN_CHUNK, (c + 1) * DN_CHUNK), slice(gidx * GW, (gidx + 1) * GW), gidx)
             for c in range(nchunk) for gidx in range(NG)]

    local = []

    def gdn_stream():
        for b0 in range(0, len(items), GDN_BATCH):
            batch = items[b0:b0 + GDN_BATCH]
            n = len(batch)
            for rows, _, _ in batch:
                rb0 = rows.start // ROW_BLK * ROW_BLK
                assert ("conv", rb0) in emitted and ("ba", rb0) in emitted
            q2 = [qn_ref[rows, lanes] for rows, lanes, _ in batch]
            k2 = [kn_ref[rows, lanes] for rows, lanes, _ in batch]
            beta2 = [bb_ref[rows, lanes] for rows, lanes, _ in batch]
            gcs = {}
            for rows, _, _ in batch:
                if rows.start not in gcs:
                    gcs[rows.start] = _sel_dot(tri, gb_ref[rows, :])
            gc = [gcs[rows.start][:, lanes] for rows, lanes, _ in batch]
            kb2 = [k2[t] * beta2[t] for t in range(n)]
            kq = [_dot_nt(jnp.concatenate([kb2[t], q2[t]], axis=0), bd(k2[t])) for t in range(n)]
            yield
            grow = [jnp.sum(jnp.where(diag, g, 0.0), axis=0, keepdims=True) for g in gc]
            dec = [jnp.exp(jnp.where(causal, gc[t] - grow[t], -jnp.inf)) for t in range(n)]
            nmat = [jnp.where(strict, -(kq[t][0:DN_CHUNK] * dec[t]), 0.0) for t in range(n)]
            amat = [kq[t][DN_CHUNK:2 * DN_CHUNK] * dec[t] for t in range(n)]
            nd = [jnp.where(sub_lo, m[0:half], m[half:DN_CHUNK]) for m in nmat]
            n21 = [jnp.where(sub_lo, m[half:DN_CHUNK], 0.0) for m in nmat]
            td = [eye4 + d for d in nd]
            pw = [_dot(d, tile4(d, diag4_mask)) for d in nd]
            yield
            for r in range(1, 5):
                if r < 4:
                    res = [_dot(pw[t], jnp.concatenate([tile4(pw[t], diag4_mask), tile4(td[t], diag4_mask)],
                                                       axis=1)) for t in range(n)]
                    pw = [x[:, 0:GW] for x in res]
                    td = [td[t] + res[t][:, GW:2 * GW] for t in range(n)]
                else:
                    td = [td[t] + _dot(pw[t], tile4(td[t], diag4_mask)) for t in range(n)]
                yield
            t22n21 = [_dot(td[t], tile4(n21[t], low_mask)) for t in range(n)]
            yield
            t21 = [_dot(t22n21[t], tile4(td[t], first_mask)) for t in range(n)]
            xinv = [jnp.concatenate([jnp.where(sub_lo, td[t], 0.0), jnp.where(sub_lo, t21[t], td[t])], axis=0)
                    for t in range(n)]
            yield
            eg = [jnp.exp(g) for g in gc]
            rhs = []
            for t in range(n):
                rows, lanes, _ = batch[t]
                rhs.append(jnp.concatenate([bd(vn_ref[rows, lanes] * beta2[t]), bd(kb2[t] * eg[t])], axis=1))
            uw = [_dot(xinv[t], rhs[t]) for t in range(n)]
            yield
            glast = [g[DN_CHUNK - 1:DN_CHUNK, :] for g in gc]
            kuw = [_dot_tn(k2[t] * jnp.exp(glast[t] - gc[t]), uw[t]) for t in range(n)]
            auw = [_dot(amat[t], jnp.concatenate([bd(uw[t][:, 0:GW]), bd(uw[t][:, GW:2 * GW])], axis=1))
                   for t in range(n)]
            yield
            for t in range(n):
                rows, lanes, gidx = batch[t]
                m2 = eye2 * jnp.exp(glast[t]) - pack_diag(kuw[t][:, GW:2 * GW])
                r2 = pack_diag(kuw[t][:, 0:GW])
                qeff = q2[t] * eg[t] - auw[t][:, GW:2 * GW]
                local.append((rows, lanes, gidx, m2, r2, qeff, auw[t][:, 0:GW]))
            yield

    def chain_stream():
        states = [s_ref[gidx] for gidx in range(NG)]
        outs = []
        for rows, lanes, gidx, m2, r2, qeff, oloc in local:
            res = _dot(jnp.concatenate([qeff, m2], axis=0), bd(states[gidx]))
            outs.append(oloc + res[0:DN_CHUNK])
            states[gidx] = res[DN_CHUNK:2 * DN_CHUNK] + r2
            if gidx == NG - 1:
                od = jnp.concatenate(outs, axis=1)
                outs = []
                mix_ref[rows, SWA_WIDTH:SWA_WIDTH + DN_WIDTH] = (
                    od * lax.rsqrt(_dot(od * od, head_ones) * (1.0 / HEAD_DIM) + EPS) * dn_g_ref[...])
                yield
        for gidx in range(NG):
            s_ref[gidx] = states[gidx]

    def mem_stream():
        for r0 in range(0, ts, MEM_ROWS):
            rows = slice(r0, r0 + MEM_ROWS)
            assert ("mq", r0) in emitted
            for p in range(MEM_HEADS // 2):
                q2 = mq_ref[rows, p * LANES:(p + 1) * LANES]
                acc = None
                for half in range(2):
                    s = _dot_nt(q2, mkv_ref[p * 2 + half])
                    yield
                    e = jnp.exp2(s - jnp.max(s, axis=-1, keepdims=True))
                    o = _dot(e, mkv_ref[4 + p * 2 + half]) * (1.0 / jnp.sum(e, axis=-1, keepdims=True))
                    acc = o if acc is None else acc + o
                    yield
                off = SWA_WIDTH + DN_WIDTH + p * LANES
                mix_ref[rows, off:off + LANES] = acc

    blk_per_rb = ROW_BLK // WINDOW
    swa_streams = [_delayed(SWA_DELAY + rb * PROJ_STEPS_PER_BLK,
                            swa_stream(range(rb * blk_per_rb, (rb + 1) * blk_per_rb)))
                   for rb in range(ts // ROW_BLK)]
    _round_robin([proj_stream(), _delayed(CONV_DELAY, conv_stream())] + swa_streams
                 + [_delayed(GDN_DELAY, gdn_stream()), _delayed(MEM_DELAY, mem_stream())])
    for ref in (kk_ref, vv_ref):
        ref[:, 0:WINDOW, :] = ref[:, ts:ts + WINDOW, :]

    def out_stream(r0):
        rows = slice(r0, r0 + OUT_ROWS)
        mixed = (mix_ref[rows, :] * _silu(gate_ref[rows, :])).astype(BF16)
        yield
        y = jnp.dot(mixed, w_out_ref[...], preferred_element_type=F32)
        yield
        out_ref[0, rows, :] = x_ref[0, rows, :] + _rms(y, post_g_ref[...])

    chunks_per_unit = OUT_ROWS // DN_CHUNK
    _round_robin([chain_stream()] + [_delayed(chunks_per_unit * (k + 1), out_stream(r0))
                                     for k, r0 in enumerate(range(0, ts, OUT_ROWS))])


def _layer(layer, x, mem, sinks, pre_g, w_in, w_ba, conv_w, alog, dtb, dn_g, mem_g, w_mem, w_out, post_g):
    b, s, d = x.shape
    ts = min(SEQ_TILE, s)
    assert s % ts == 0 and ts % PROJ_ROWS == 0 and PROJ_ROWS % ROW_BLK == 0 and d == D_MODEL
    this_layer = lambda bi, si: (layer, 0, 0)
    resident = pl.Buffered(1)
    return pl.pallas_call(
        functools.partial(_layer_kernel, ts=ts, layer=layer),
        out_shape=jax.ShapeDtypeStruct(x.shape, x.dtype),
        grid=(b, s // ts),
        in_specs=[
            pl.BlockSpec(memory_space=pltpu.SMEM),
            pl.BlockSpec((1, ts, d), lambda bi, si: (bi, si, 0)),
            pl.BlockSpec((1, N_MEM, d), lambda bi, si: (bi, 0, 0)),
            pl.BlockSpec((None, 1, d), this_layer),
            pl.BlockSpec((None, d, IN_COLS), this_layer, pipeline_mode=resident),
            pl.BlockSpec((None, BA_ROWS, d), this_layer),
            pl.BlockSpec((None, DN_CONV, 3 * DN_WIDTH), this_layer),
            pl.BlockSpec((None, BA_ROWS, 1), this_layer),
            pl.BlockSpec((None, BA_ROWS, 1), this_layer),
            pl.BlockSpec((None, 1, DN_WIDTH), this_layer),
            pl.BlockSpec((None, 1, d), this_layer),
            pl.BlockSpec((None, d, 2 * MEM_WIDTH), this_layer, pipeline_mode=resident),
            pl.BlockSpec((None, MIX_WIDTH, d), this_layer, pipeline_mode=resident),
            pl.BlockSpec((None, 1, d), this_layer),
        ],
        out_specs=pl.BlockSpec((1, ts, d), lambda bi, si: (bi, si, 0)),
        scratch_shapes=[
            pltpu.VMEM((ts, d), BF16),
            pltpu.VMEM((ts, SWA_WIDTH), BF16),
            pltpu.VMEM((4, WINDOW + ts, LANES), BF16),
            pltpu.VMEM((4, WINDOW + ts, LANES), BF16),
            pltpu.VMEM((CONV_HIST + ts, 3 * DN_WIDTH), F32),
            pltpu.VMEM((ts, DN_WIDTH), F32),
            pltpu.VMEM((ts, DN_WIDTH), F32),
            pltpu.VMEM((ts, DN_WIDTH), F32),
            pltpu.VMEM((ts, DN_WIDTH), F32),
            pltpu.VMEM((ts, DN_WIDTH), F32),
            pltpu.VMEM((NG, DN_CHUNK, GW), F32),
            pltpu.VMEM((8, N_MEM, LANES), BF16),
            pltpu.VMEM((ts, MEM_WIDTH), BF16),
            pltpu.VMEM((ts, MIX_WIDTH), F32),
            pltpu.VMEM((ts, MIX_WIDTH), F32),
        ],
        compiler_params=pltpu.CompilerParams(
            dimension_semantics=("arbitrary", "arbitrary"),
            vmem_limit_bytes=VMEM_LIMIT_BYTES),
        name="hybrid_layer",
    )(sinks, x, mem, pre_g, w_in, w_ba, conv_w, alog, dtb, dn_g, mem_g, w_mem, w_out, post_g)


_BA_COL = SWA_WIDTH + 2 * LANES + 3 * DN_WIDTH
_AFTER_BA = _BA_COL + 2 * DN_HEADS


def _prep_w_in(w_in):
    main = jnp.concatenate([w_in[:, :, 0:_BA_COL], w_in[:, :, _AFTER_BA:]], axis=2)
    col = lax.broadcasted_iota(jnp.int32, (1, 1, IN_COLS), 2)
    is_query = (col < SWA_WIDTH) | ((col >= C_MQ) & (col < C_MQ + MEM_WIDTH))
    main = main * jnp.where(is_query, SCALE * LOG2E, 1.0)
    w_ba = jnp.swapaxes(w_in[:, :, _BA_COL:_AFTER_BA], 1, 2)
    w_ba = jnp.pad(w_ba, ((0, 0), (0, BA_ROWS - 2 * DN_HEADS), (0, 0)))
    return main.astype(BF16), w_ba.astype(BF16)


def _row_place(v, start):
    return jnp.pad(v.astype(F32), ((0, 0), (start, BA_ROWS - start - v.shape[1])))[:, :, None]


@jax.jit
def kernel(x, mem, pre_norm_g, w_in, conv_w, a_log, dt_bias, sinks, dn_norm_g, mem_norm_g, w_mem_kv,
           w_out, post_norm_g):
    w_main, w_ba = _prep_w_in(w_in)
    params = (sinks.astype(F32), pre_norm_g[:, None, :], w_main, w_ba, conv_w,
              _row_place(a_log, DN_HEADS), _row_place(dt_bias, DN_HEADS),
              jnp.tile(dn_norm_g, (1, DN_HEADS))[:, None, :], mem_norm_g[:, None, :],
              w_mem_kv.astype(BF16), w_out.astype(BF16), post_norm_g[:, None, :])
    for layer in range(w_in.shape[0]):
        x = _layer(layer, x, mem, *params)
    return x
```

```python
import functools

import jax
import jax.numpy as jnp
from jax import lax
from jax.experimental import pallas as pl
from jax.experimental.pallas import tpu as pltpu

F32 = jnp.float32
BF16 = jnp.bfloat16

D_MODEL = 1024
HEAD_DIM = 64
SWA_HEADS = 8
SWA_KV_HEADS = 2
SWA_WIDTH = SWA_HEADS * HEAD_DIM
WINDOW = 128
DN_HEADS = 4
DN_WIDTH = DN_HEADS * HEAD_DIM
DN_CONV = 4
DN_CHUNK = 64
N_MEM = 256
MEM_HEADS = 4
MEM_WIDTH = MEM_HEADS * HEAD_DIM
MIX_WIDTH = 1024
EPS = 1e-6
SCALE = HEAD_DIM ** -0.5
LOG2E = 1.4426950408889634

LANES = 128
MXU_N = 256
CONV_HIST = 8

C_SWA = 0
C_DN = 768
C_MQ = 1536
C_GATE = 1792
IN_COLS = 2816
BA_ROWS = 16

GW = 128
NG = DN_WIDTH // GW
HPG = GW // HEAD_DIM

SEQ_TILE = 1024
GDN_BATCH = 8
ROW_BLK = 256
MEM_ROWS = ROW_BLK
PERIOD = 12
CONV_DELAY = 3
SWA_DELAY = 6
GDN_DELAY = 7
GDN_STEPS = 10
MEM_DELAY = 10
MEM_STEPS = 8
CHAIN_DELAY = GDN_DELAY + GDN_STEPS
CHAIN_STEPS = ROW_BLK // DN_CHUNK
OUT_DELAY = 22
VMEM_LIMIT_BYTES = 60 * 1024 * 1024


def _dot(a, b):
    return jnp.dot(a.astype(BF16), b.astype(BF16), preferred_element_type=F32)


def _dot_nt(a, b):
    return lax.dot_general(a.astype(BF16), b.astype(BF16), (((1,), (1,)), ((), ())),
                           preferred_element_type=F32)


def _dot_tn(a, b):
    return lax.dot_general(a.astype(BF16), b.astype(BF16), (((0,), (0,)), ((), ())),
                           preferred_element_type=F32)


def _split2(x):
    x0 = x.astype(BF16)
    x1 = (x - x0.astype(F32)).astype(BF16)
    return x0, x1


def _sel_dot(m, x):
    x0, x1 = _split2(x)
    return jnp.dot(m, x0, preferred_element_type=F32) + jnp.dot(m, x1, preferred_element_type=F32)


def _iota(shape, dim):
    return lax.broadcasted_iota(jnp.int32, shape, dim)


def _rms(x, g):
    return x * lax.rsqrt(jnp.mean(x * x, axis=-1, keepdims=True) + EPS) * g


def _silu(x):
    hx = 0.5 * x
    return hx + hx * jnp.tanh(hx)


def _round_robin(streams):
    streams = list(streams)
    while streams:
        for s in list(streams):
            try:
                next(s)
            except StopIteration:
                streams.remove(s)


def _delayed(n, gen):
    for _ in range(n):
        yield
    yield from gen


def _layer_kernel(sinks_ref, x_ref, mem_ref, pre_g_ref, w_in_ref, w_ba_ref, conv_w_ref, alog_ref, dtb_ref,
                  dn_g_ref, mem_g_ref, w_mem_ref, w_out_ref, post_g_ref, out_ref,
                  h_ref, sq_ref, kk_ref, vv_ref, dnp_ref, qn_ref, kn_ref, vn_ref, gb_ref, bb_ref,
                  s_ref, mkv_ref, mq_ref, gate_ref, mix_ref, *, ts, layer):
    i = pl.program_id(1)
    nblk = ts // WINDOW
    nchunk = ts // DN_CHUNK

    lane = _iota((1, LANES), 1)
    lo = lane < HEAD_DIM
    emitted = set()

    @pl.when(i == 0)
    def _():
        s_ref[...] = jnp.zeros_like(s_ref)
        kk_ref[:, 0:WINDOW, :] = jnp.zeros((4, WINDOW, LANES), BF16)
        vv_ref[:, 0:WINDOW, :] = jnp.zeros((4, WINDOW, LANES), BF16)
        dnp_ref[0:CONV_HIST, :] = jnp.zeros((CONV_HIST, 3 * DN_WIDTH), F32)
        hm = _rms(mem_ref[0], mem_g_ref[...])
        mkv = _dot(hm, w_mem_ref[...])
        for kind in range(2):
            for p in range(MEM_HEADS // 2):
                blk = mkv[:, kind * MEM_WIDTH + p * LANES: kind * MEM_WIDTH + (p + 1) * LANES]
                mkv_ref[kind * 4 + p * 2 + 0] = jnp.where(lo, blk, 0.0).astype(BF16)
                mkv_ref[kind * 4 + p * 2 + 1] = jnp.where(lo, 0.0, blk).astype(BF16)

    h_ref[...] = _rms(x_ref[0], pre_g_ref[...]).astype(BF16)

    def proj(c0, width, rows):
        return jnp.dot(h_ref[rows, :], w_in_ref[:, c0:c0 + width], preferred_element_type=F32)

    sel_r = _iota((BA_ROWS, DN_WIDTH), 0)
    sel_c = _iota((BA_ROWS, DN_WIDTH), 1) // HEAD_DIM
    beta_sel = (sel_r == sel_c).astype(BF16)
    decay_sel = (sel_r == DN_HEADS + sel_c).astype(BF16)

    def proj_stream():
        for r0 in range(0, ts, ROW_BLK):
            rows = slice(r0, r0 + ROW_BLK)
            hist_rows = slice(CONV_HIST + r0, CONV_HIST + r0 + ROW_BLK)
            kv_rows = slice(WINDOW + r0, WINDOW + r0 + ROW_BLK)
            for c in range(3 * DN_WIDTH // MXU_N):
                dnp_ref[hist_rows, c * MXU_N:(c + 1) * MXU_N] = proj(C_DN + c * MXU_N, MXU_N, rows)
                emitted.add(("dn", r0, c))
                yield
            for c in range(SWA_WIDTH // MXU_N):
                sq_ref[rows, c * MXU_N:(c + 1) * MXU_N] = proj(C_SWA + c * MXU_N, MXU_N, rows).astype(BF16)
                yield
            kvcur = proj(C_SWA + SWA_WIDTH, 2 * LANES, rows)
            for ref, cur in ((kk_ref, kvcur[:, 0:LANES]), (vv_ref, kvcur[:, LANES:2 * LANES])):
                rolled = pltpu.roll(cur, HEAD_DIM, 1)
                ref[0, kv_rows, :] = jnp.where(lo, cur, 0.0).astype(BF16)
                ref[1, kv_rows, :] = jnp.where(lo, 0.0, rolled).astype(BF16)
                ref[2, kv_rows, :] = jnp.where(lo, rolled, 0.0).astype(BF16)
                ref[3, kv_rows, :] = jnp.where(lo, 0.0, cur).astype(BF16)
            emitted.add(("qkv", r0))
            yield
            ba = _dot_nt(w_ba_ref[...], h_ref[rows, :])
            beta = 1.0 / (1.0 + jnp.exp(-ba))
            z = ba + dtb_ref[...]
            softplus = jnp.maximum(z, 0.0) + jnp.log(1.0 + jnp.exp(-jnp.abs(z)))
            glog = -jnp.exp(alog_ref[...]) * softplus
            bb_ref[rows, :] = _dot_tn(beta, beta_sel)
            g0, g1 = _split2(glog)
            gb_ref[rows, :] = _dot_tn(g0, decay_sel) + _dot_tn(g1, decay_sel)
            emitted.add(("ba", r0))
            yield
            mq_ref[rows, :] = proj(C_MQ, MEM_WIDTH, rows).astype(BF16)
            emitted.add(("mq", r0))
            yield
            for c in range(MIX_WIDTH // MXU_N):
                gate_ref[rows, c * MXU_N:(c + 1) * MXU_N] = proj(C_GATE + c * MXU_N, MXU_N, rows)
                if c == MIX_WIDTH // MXU_N - 1:
                    emitted.add(("gate", r0))
                yield

    from_prev = _iota((2 * WINDOW, WINDOW), 1) > (_iota((2 * WINDOW, WINDOW), 0) % WINDOW)
    prev_mask = from_prev.astype(BF16)
    cur_mask = 1.0 - prev_mask
    upper = _iota((2 * WINDOW, 1), 0) < WINDOW
    group = SWA_HEADS // SWA_KV_HEADS

    def swa_stream(blocks):
        for j in blocks:
            assert ("qkv", j * WINDOW // ROW_BLK * ROW_BLK) in emitted
            rows = slice(j * WINDOW, (j + 1) * WINDOW)
            krows = slice(j * WINDOW, (j + 2) * WINDOW)
            for kvh in range(SWA_KV_HEADS):
                p0 = kvh * (group // 2)
                q_st = jnp.concatenate([sq_ref[rows, p0 * LANES:(p0 + 1) * LANES],
                                        sq_ref[rows, (p0 + 1) * LANES:(p0 + 2) * LANES]], axis=0)
                acc = None
                for half in range(2):
                    s = _dot_nt(q_st, kk_ref[2 * kvh + half, krows, :])
                    yield
                    s_prev = s[:, 0:WINDOW]
                    if j == 0:
                        s_prev = jnp.where(i > 0, s_prev, -jnp.inf)
                    s = jnp.where(from_prev, s_prev, s[:, WINDOW:2 * WINDOW])
                    sink = jnp.where(upper, sinks_ref[layer, 2 * p0 + half],
                                     sinks_ref[layer, 2 * p0 + 2 + half]) * LOG2E
                    m = jnp.maximum(jnp.max(s, axis=-1, keepdims=True), sink)
                    e = jnp.exp2(s - m)
                    denom = jnp.sum(e, axis=-1, keepdims=True) + jnp.exp2(sink - m)
                    eb = e.astype(BF16)
                    e2 = jnp.concatenate([eb * prev_mask, eb * cur_mask], axis=1)
                    o = _dot(e2, vv_ref[2 * kvh + half, krows, :]) * (1.0 / denom)
                    acc = o if acc is None else acc + o
                    yield
                mix_ref[rows, p0 * LANES:(p0 + 1) * LANES] = acc[0:WINDOW]
                mix_ref[rows, (p0 + 1) * LANES:(p0 + 2) * LANES] = acc[WINDOW:2 * WINDOW]
            emitted.add(("swa", j))

    head_ones = ((_iota((DN_WIDTH, DN_WIDTH), 0) // HEAD_DIM)
                 == (_iota((DN_WIDTH, DN_WIDTH), 1) // HEAD_DIM)).astype(BF16)

    def conv_stream():
        for r0 in range(0, ts, ROW_BLK):
            rows = slice(r0, r0 + ROW_BLK)
            for part, dst in enumerate((qn_ref, kn_ref, vn_ref)):
                cols = slice(part * DN_WIDTH, (part + 1) * DN_WIDTH)
                assert ("dn", r0, part) in emitted
                xs = dnp_ref[r0:r0 + CONV_HIST + ROW_BLK, cols]
                acc = xs * conv_w_ref[0:1, cols]
                for j in range(1, DN_CONV):
                    acc = pltpu.roll(acc, 1, 0) + xs * conv_w_ref[j:j + 1, cols]
                conv = _silu(acc[CONV_HIST:CONV_HIST + ROW_BLK, :])
                if part == 0:
                    dst[rows, :] = conv * (lax.rsqrt(_dot(conv * conv, head_ones) + EPS) * SCALE)
                elif part == 1:
                    dst[rows, :] = conv * lax.rsqrt(_dot(conv * conv, head_ones) + EPS)
                else:
                    dst[rows, :] = conv
                yield
            emitted.add(("conv", r0))
            for _ in range(PERIOD - 3):
                yield
        dnp_ref[0:CONV_HIST, :] = dnp_ref[ts:ts + CONV_HIST, :]

    cc = _iota((DN_CHUNK, GW), 0)
    cs = _iota((DN_CHUNK, GW), 1) % HEAD_DIM
    causal = cc >= cs
    strict = cc > cs
    diag = cc == cs
    eye2 = diag.astype(F32)
    lo_g = _iota((DN_CHUNK, GW), 1) < HEAD_DIM
    tri = (_iota((DN_CHUNK, DN_CHUNK), 0) >= _iota((DN_CHUNK, DN_CHUNK), 1)).astype(BF16)
    bd_mask = ((_iota((GW, GW), 0) // HEAD_DIM) == (_iota((GW, GW), 1) // HEAD_DIM)).astype(BF16)

    def bd(x):
        xb = x.astype(BF16)
        return jnp.concatenate([xb] * HPG, axis=0) * bd_mask

    def pack_diag(x):
        return jnp.where(lo_g, x[0:HEAD_DIM], x[HEAD_DIM:2 * HEAD_DIM])

    items = [(slice(c * DN_CHUNK, (c + 1) * DN_CHUNK), slice(gidx * GW, (gidx + 1) * GW), gidx)
             for c in range(nchunk) for gidx in range(NG)]

    local = []

    def gdn_stream():
        for b0 in range(0, len(items), GDN_BATCH):
            batch = items[b0:b0 + GDN_BATCH]
            n = len(batch)
            for rows, _, _ in batch:
                rb0 = rows.start // ROW_BLK * ROW_BLK
                assert ("conv", rb0) in emitted and ("ba", rb0) in emitted
            q2 = [qn_ref[rows, lanes] for rows, lanes, _ in batch]
            k2 = [kn_ref[rows, lanes] for rows, lanes, _ in batch]
            beta2 = [bb_ref[rows, lanes] for rows, lanes, _ in batch]
            gcs = {}
            for rows, _, _ in batch:
                if rows.start not in gcs:
                    gcs[rows.start] = _sel_dot(tri, gb_ref[rows, :])
            gc = [gcs[rows.start][:, lanes] for rows, lanes, _ in batch]
            kb2 = [k2[t] * beta2[t] for t in range(n)]
            kq = [_dot_nt(jnp.concatenate([kb2[t], q2[t]], axis=0), bd(k2[t])) for t in range(n)]
            yield
            grow = [jnp.sum(jnp.where(diag, g, 0.0), axis=0, keepdims=True) for g in gc]
            dec = [jnp.exp(jnp.where(causal, gc[t] - grow[t], -jnp.inf)) for t in range(n)]
            nmat = [jnp.where(strict, -(kq[t][0:DN_CHUNK] * dec[t]), 0.0) for t in range(n)]
            amat = [kq[t][DN_CHUNK:2 * DN_CHUNK] * dec[t] for t in range(n)]
            xinv = [eye2 + m for m in nmat]
            pw = [_dot(m, bd(m)) for m in nmat]
            yield
            for r in range(1, 6):
                if r < 5:
                    res = [_dot(pw[t], jnp.concatenate([bd(pw[t]), bd(xinv[t])], axis=1)) for t in range(n)]
                    pw = [x[:, 0:GW] for x in res]
                    xinv = [xinv[t] + res[t][:, GW:2 * GW] for t in range(n)]
                else:
                    xinv = [xinv[t] + _dot(pw[t], bd(xinv[t])) for t in range(n)]
                yield
            eg = [jnp.exp(g) for g in gc]
            rhs = []
            for t in range(n):
                rows, lanes, _ = batch[t]
                rhs.append(jnp.concatenate([bd(vn_ref[rows, lanes] * beta2[t]), bd(kb2[t] * eg[t])], axis=1))
            uw = [_dot(xinv[t], rhs[t]) for t in range(n)]
            yield
            glast = [g[DN_CHUNK - 1:DN_CHUNK, :] for g in gc]
            kuw = [_dot_tn(k2[t] * jnp.exp(glast[t] - gc[t]), uw[t]) for t in range(n)]
            auw = [_dot(amat[t], jnp.concatenate([bd(uw[t][:, 0:GW]), bd(uw[t][:, GW:2 * GW])], axis=1))
                   for t in range(n)]
            yield
            for t in range(n):
                rows, lanes, gidx = batch[t]
                m2 = eye2 * jnp.exp(glast[t]) - pack_diag(kuw[t][:, GW:2 * GW])
                r2 = pack_diag(kuw[t][:, 0:GW])
                qeff = q2[t] * eg[t] - auw[t][:, GW:2 * GW]
                local.append((rows, lanes, gidx, m2, r2, qeff, auw[t][:, 0:GW]))
            emitted.add(("gdn", b0))
            for _ in range(PERIOD - GDN_STEPS + 1):
                yield

    def chain_stream():
        states = [s_ref[gidx] for gidx in range(NG)]
        for b0 in range(0, len(items), GDN_BATCH):
            assert ("gdn", b0) in emitted
            outs = []
            for rows, lanes, gidx, m2, r2, qeff, oloc in local[b0:b0 + GDN_BATCH]:
                res = _dot(jnp.concatenate([qeff, m2], axis=0), bd(states[gidx]))
                outs.append(oloc + res[0:DN_CHUNK])
                states[gidx] = res[DN_CHUNK:2 * DN_CHUNK] + r2
                if gidx == NG - 1:
                    od = jnp.concatenate(outs, axis=1)
                    outs = []
                    mix_ref[rows, SWA_WIDTH:SWA_WIDTH + DN_WIDTH] = (
                        od * lax.rsqrt(_dot(od * od, head_ones) * (1.0 / HEAD_DIM) + EPS) * dn_g_ref[...])
                    emitted.add(("chain", rows.start))
                    yield
            for _ in range(PERIOD - CHAIN_STEPS):
                yield
        for gidx in range(NG):
            s_ref[gidx] = states[gidx]

    def mem_stream():
        for r0 in range(0, ts, MEM_ROWS):
            rows = slice(r0, r0 + MEM_ROWS)
            assert ("mq", r0) in emitted
            for p in range(MEM_HEADS // 2):
                q2 = mq_ref[rows, p * LANES:(p + 1) * LANES]
                acc = None
                for half in range(2):
                    s = _dot_nt(q2, mkv_ref[p * 2 + half])
                    yield
                    e = jnp.exp2(s - jnp.max(s, axis=-1, keepdims=True))
                    o = _dot(e, mkv_ref[4 + p * 2 + half]) * (1.0 / jnp.sum(e, axis=-1, keepdims=True))
                    acc = o if acc is None else acc + o
                    yield
                off = SWA_WIDTH + DN_WIDTH + p * LANES
                mix_ref[rows, off:off + LANES] = acc
            emitted.add(("mem", r0))
            for _ in range(PERIOD - MEM_STEPS):
                yield

    def out_stream(r0):
        rows = slice(r0, r0 + ROW_BLK)
        assert ("gate", r0) in emitted and ("mem", r0) in emitted
        assert all(("swa", j) in emitted for j in range(r0 // WINDOW, (r0 + ROW_BLK) // WINDOW))
        assert all(("chain", c0) in emitted for c0 in range(r0, r0 + ROW_BLK, DN_CHUNK))
        mixed = (mix_ref[rows, :] * _silu(gate_ref[rows, :])).astype(BF16)
        yield
        y = jnp.dot(mixed, w_out_ref[...], preferred_element_type=F32)
        yield
        out_ref[0, rows, :] = x_ref[0, rows, :] + _rms(y, post_g_ref[...])

    blk_per_rb = ROW_BLK // WINDOW
    nrb = ts // ROW_BLK
    swa_streams = [_delayed(SWA_DELAY + rb * PERIOD, swa_stream(range(rb * blk_per_rb, (rb + 1) * blk_per_rb)))
                   for rb in range(nrb)]
    out_streams = [_delayed(OUT_DELAY + rb * PERIOD, out_stream(rb * ROW_BLK)) for rb in range(nrb)]
    _round_robin([proj_stream(), _delayed(CONV_DELAY, conv_stream())] + swa_streams
                 + [_delayed(GDN_DELAY, gdn_stream()), _delayed(MEM_DELAY, mem_stream()),
                    _delayed(CHAIN_DELAY, chain_stream())] + out_streams)
    for ref in (kk_ref, vv_ref):
        ref[:, 0:WINDOW, :] = ref[:, ts:ts + WINDOW, :]


def _layer(layer, x, mem, sinks, pre_g, w_in, w_ba, conv_w, alog, dtb, dn_g, mem_g, w_mem, w_out, post_g):
    b, s, d = x.shape
    ts = min(SEQ_TILE, s)
    assert s % ts == 0 and ts % ROW_BLK == 0 and d == D_MODEL
    this_layer = lambda bi, si: (layer, 0, 0)
    resident = pl.Buffered(1)
    return pl.pallas_call(
        functools.partial(_layer_kernel, ts=ts, layer=layer),
        out_shape=jax.ShapeDtypeStruct(x.shape, x.dtype),
        grid=(b, s // ts),
        in_specs=[
            pl.BlockSpec(memory_space=pltpu.SMEM),
            pl.BlockSpec((1, ts, d), lambda bi, si: (bi, si, 0)),
            pl.BlockSpec((1, N_MEM, d), lambda bi, si: (bi, 0, 0)),
            pl.BlockSpec((None, 1, d), this_layer),
            pl.BlockSpec((None, d, IN_COLS), this_layer, pipeline_mode=resident),
            pl.BlockSpec((None, BA_ROWS, d), this_layer),
            pl.BlockSpec((None, DN_CONV, 3 * DN_WIDTH), this_layer),
            pl.BlockSpec((None, BA_ROWS, 1), this_layer),
            pl.BlockSpec((None, BA_ROWS, 1), this_layer),
            pl.BlockSpec((None, 1, DN_WIDTH), this_layer),
            pl.BlockSpec((None, 1, d), this_layer),
            pl.BlockSpec((None, d, 2 * MEM_WIDTH), this_layer, pipeline_mode=resident),
            pl.BlockSpec((None, MIX_WIDTH, d), this_layer, pipeline_mode=resident),
            pl.BlockSpec((None, 1, d), this_layer),
        ],
        out_specs=pl.BlockSpec((1, ts, d), lambda bi, si: (bi, si, 0)),
        scratch_shapes=[
            pltpu.VMEM((ts, d), BF16),
            pltpu.VMEM((ts, SWA_WIDTH), BF16),
            pltpu.VMEM((4, WINDOW + ts, LANES), BF16),
            pltpu.VMEM((4, WINDOW + ts, LANES), BF16),
            pltpu.VMEM((CONV_HIST + ts, 3 * DN_WIDTH), F32),
            pltpu.VMEM((ts, DN_WIDTH), F32),
            pltpu.VMEM((ts, DN_WIDTH), F32),
            pltpu.VMEM((ts, DN_WIDTH), F32),
            pltpu.VMEM((ts, DN_WIDTH), F32),
            pltpu.VMEM((ts, DN_WIDTH), F32),
            pltpu.VMEM((NG, DN_CHUNK, GW), F32),
            pltpu.VMEM((8, N_MEM, LANES), BF16),
            pltpu.VMEM((ts, MEM_WIDTH), BF16),
            pltpu.VMEM((ts, MIX_WIDTH), F32),
            pltpu.VMEM((ts, MIX_WIDTH), F32),
        ],
        compiler_params=pltpu.CompilerParams(
            dimension_semantics=("arbitrary", "arbitrary"),
            vmem_limit_bytes=VMEM_LIMIT_BYTES),
        name="hybrid_layer",
    )(sinks, x, mem, pre_g, w_in, w_ba, conv_w, alog, dtb, dn_g, mem_g, w_mem, w_out, post_g)


_BA_COL = SWA_WIDTH + 2 * LANES + 3 * DN_WIDTH
_AFTER_BA = _BA_COL + 2 * DN_HEADS


def _prep_w_in(w_in):
    main = jnp.concatenate([w_in[:, :, 0:_BA_COL], w_in[:, :, _AFTER_BA:]], axis=2)
    col = lax.broadcasted_iota(jnp.int32, (1, 1, IN_COLS), 2)
    is_query = (col < SWA_WIDTH) | ((col >= C_MQ) & (col < C_MQ + MEM_WIDTH))
    main = main * jnp.where(is_query, SCALE * LOG2E, 1.0)
    w_ba = jnp.swapaxes(w_in[:, :, _BA_COL:_AFTER_BA], 1, 2)
    w_ba = jnp.pad(w_ba, ((0, 0), (0, BA_ROWS - 2 * DN_HEADS), (0, 0)))
    return main.astype(BF16), w_ba.astype(BF16)


def _row_place(v, start):
    return jnp.pad(v.astype(F32), ((0, 0), (start, BA_ROWS - start - v.shape[1])))[:, :, None]


@jax.jit
def kernel(x, mem, pre_norm_g, w_in, conv_w, a_log, dt_bias, sinks, dn_norm_g, mem_norm_g, w_mem_kv,
           w_out, post_norm_g):
    w_main, w_ba = _prep_w_in(w_in)
    params = (sinks.astype(F32), pre_norm_g[:, None, :], w_main, w_ba, conv_w,
              _row_place(a_log, DN_HEADS), _row_place(dt_bias, DN_HEADS),
              jnp.tile(dn_norm_g, (1, DN_HEADS))[:, None, :], mem_norm_g[:, None, :],
              w_mem_kv.astype(BF16), w_out.astype(BF16), post_norm_g[:, None, :])
    for layer in range(w_in.shape[0]):
        x = _layer(layer, x, mem, *params)
    return x
```

```python
import functools

import jax
import jax.numpy as jnp
from jax import lax
from jax.experimental import pallas as pl
from jax.experimental.pallas import tpu as pltpu

F32 = jnp.float32
BF16 = jnp.bfloat16

D_MODEL = 1024
HEAD_DIM = 64
SWA_HEADS = 8
SWA_KV_HEADS = 2
SWA_WIDTH = SWA_HEADS * HEAD_DIM
WINDOW = 128
DN_HEADS = 4
DN_WIDTH = DN_HEADS * HEAD_DIM
DN_CONV = 4
DN_CHUNK = 64
N_MEM = 256
MEM_HEADS = 4
MEM_WIDTH = MEM_HEADS * HEAD_DIM
MIX_WIDTH = 1024
EPS = 1e-6
SCALE = HEAD_DIM ** -0.5
LOG2E = 1.4426950408889634

LANES = 128
MXU_N = 256
CONV_HIST = 8

C_SWA = 0
C_DN = 768
C_MQ = 1536
C_GATE = 1792
IN_COLS = 2816
BA_ROWS = 16

GW = 128
NG = DN_WIDTH // GW
HPG = GW // HEAD_DIM

SEQ_TILE = 1024
GDN_BATCH = 8
ROW_BLK = 256
MEM_ROWS = ROW_BLK
PERIOD = 13
CONV_DELAY = 4
SWA_DELAY = 7
GDN_DELAY = 8
GDN_STEPS = 10
MEM_DELAY = 11
MEM_STEPS = 8
CHAIN_DELAY = GDN_DELAY + GDN_STEPS
CHAIN_STEPS = ROW_BLK // DN_CHUNK
OUT_DELAY = 23
VMEM_LIMIT_BYTES = 60 * 1024 * 1024


def _dot(a, b):
    return jnp.dot(a.astype(BF16), b.astype(BF16), preferred_element_type=F32)


def _dot_nt(a, b):
    return lax.dot_general(a.astype(BF16), b.astype(BF16), (((1,), (1,)), ((), ())),
                           preferred_element_type=F32)


def _dot_tn(a, b):
    return lax.dot_general(a.astype(BF16), b.astype(BF16), (((0,), (0,)), ((), ())),
                           preferred_element_type=F32)


def _split2(x):
    x0 = x.astype(BF16)
    x1 = (x - x0.astype(F32)).astype(BF16)
    return x0, x1


def _sel_dot(m, x):
    x0, x1 = _split2(x)
    return jnp.dot(m, x0, preferred_element_type=F32) + jnp.dot(m, x1, preferred_element_type=F32)


def _iota(shape, dim):
    return lax.broadcasted_iota(jnp.int32, shape, dim)


def _rms(x, g):
    return x * lax.rsqrt(jnp.mean(x * x, axis=-1, keepdims=True) + EPS) * g


def _silu(x):
    hx = 0.5 * x
    return hx + hx * jnp.tanh(hx)


def _round_robin(streams):
    streams = list(streams)
    while streams:
        for s in list(streams):
            try:
                next(s)
            except StopIteration:
                streams.remove(s)


def _delayed(n, gen):
    for _ in range(n):
        yield
    yield from gen


def _layer_kernel(sinks_ref, x_ref, mem_ref, pre_g_ref, w_in_ref, w_ba_ref, conv_w_ref, alog_ref, dtb_ref,
                  dn_g_ref, mem_g_ref, w_mem_ref, w_out_ref, post_g_ref, out_ref,
                  h_ref, sq_ref, kk_ref, vv_ref, dnp_ref, qn_ref, kn_ref, vn_ref, gb_ref, bb_ref,
                  s_ref, mkv_ref, mq_ref, gate_ref, mix_ref, *, ts, layer):
    i = pl.program_id(1)
    nblk = ts // WINDOW
    nchunk = ts // DN_CHUNK

    lane = _iota((1, LANES), 1)
    lo = lane < HEAD_DIM
    emitted = set()

    @pl.when(i == 0)
    def _():
        s_ref[...] = jnp.zeros_like(s_ref)
        kk_ref[:, 0:WINDOW, :] = jnp.zeros((4, WINDOW, LANES), BF16)
        vv_ref[:, 0:WINDOW, :] = jnp.zeros((4, WINDOW, LANES), BF16)
        dnp_ref[0:CONV_HIST, :] = jnp.zeros((CONV_HIST, 3 * DN_WIDTH), F32)
        hm = _rms(mem_ref[0], mem_g_ref[...])
        mkv = _dot(hm, w_mem_ref[...])
        for kind in range(2):
            for p in range(MEM_HEADS // 2):
                blk = mkv[:, kind * MEM_WIDTH + p * LANES: kind * MEM_WIDTH + (p + 1) * LANES]
                mkv_ref[kind * 4 + p * 2 + 0] = jnp.where(lo, blk, 0.0).astype(BF16)
                mkv_ref[kind * 4 + p * 2 + 1] = jnp.where(lo, 0.0, blk).astype(BF16)

    def proj(c0, width, rows):
        return jnp.dot(h_ref[rows, :], w_in_ref[:, c0:c0 + width], preferred_element_type=F32)

    sel_r = _iota((BA_ROWS, DN_WIDTH), 0)
    sel_c = _iota((BA_ROWS, DN_WIDTH), 1) // HEAD_DIM
    beta_sel = (sel_r == sel_c).astype(BF16)
    decay_sel = (sel_r == DN_HEADS + sel_c).astype(BF16)

    def proj_stream():
        for r0 in range(0, ts, ROW_BLK):
            rows = slice(r0, r0 + ROW_BLK)
            hist_rows = slice(CONV_HIST + r0, CONV_HIST + r0 + ROW_BLK)
            kv_rows = slice(WINDOW + r0, WINDOW + r0 + ROW_BLK)
            h_ref[rows, :] = _rms(x_ref[0, rows, :], pre_g_ref[...]).astype(BF16)
            yield
            for c in range(3 * DN_WIDTH // MXU_N):
                dnp_ref[hist_rows, c * MXU_N:(c + 1) * MXU_N] = proj(C_DN + c * MXU_N, MXU_N, rows)
                emitted.add(("dn", r0, c))
                yield
            for c in range(SWA_WIDTH // MXU_N):
                sq_ref[rows, c * MXU_N:(c + 1) * MXU_N] = proj(C_SWA + c * MXU_N, MXU_N, rows).astype(BF16)
                yield
            kvcur = proj(C_SWA + SWA_WIDTH, 2 * LANES, rows)
            for ref, cur in ((kk_ref, kvcur[:, 0:LANES]), (vv_ref, kvcur[:, LANES:2 * LANES])):
                rolled = pltpu.roll(cur, HEAD_DIM, 1)
                ref[0, kv_rows, :] = jnp.where(lo, cur, 0.0).astype(BF16)
                ref[1, kv_rows, :] = jnp.where(lo, 0.0, rolled).astype(BF16)
                ref[2, kv_rows, :] = jnp.where(lo, rolled, 0.0).astype(BF16)
                ref[3, kv_rows, :] = jnp.where(lo, 0.0, cur).astype(BF16)
            emitted.add(("qkv", r0))
            yield
            ba = _dot_nt(w_ba_ref[...], h_ref[rows, :])
            beta = 1.0 / (1.0 + jnp.exp(-ba))
            z = ba + dtb_ref[...]
            softplus = jnp.maximum(z, 0.0) + jnp.log(1.0 + jnp.exp(-jnp.abs(z)))
            glog = -jnp.exp(alog_ref[...]) * softplus
            bb_ref[rows, :] = _dot_tn(beta, beta_sel)
            g0, g1 = _split2(glog)
            gb_ref[rows, :] = _dot_tn(g0, decay_sel) + _dot_tn(g1, decay_sel)
            emitted.add(("ba", r0))
            yield
            mq_ref[rows, :] = proj(C_MQ, MEM_WIDTH, rows).astype(BF16)
            emitted.add(("mq", r0))
            yield
            for c in range(MIX_WIDTH // MXU_N):
                gate_ref[rows, c * MXU_N:(c + 1) * MXU_N] = proj(C_GATE + c * MXU_N, MXU_N, rows)
                if c == MIX_WIDTH // MXU_N - 1:
                    emitted.add(("gate", r0))
                yield

    from_prev = _iota((2 * WINDOW, WINDOW), 1) > (_iota((2 * WINDOW, WINDOW), 0) % WINDOW)
    prev_mask = from_prev.astype(BF16)
    cur_mask = 1.0 - prev_mask
    upper = _iota((2 * WINDOW, 1), 0) < WINDOW
    group = SWA_HEADS // SWA_KV_HEADS

    def swa_stream(blocks):
        for j in blocks:
            assert ("qkv", j * WINDOW // ROW_BLK * ROW_BLK) in emitted
            rows = slice(j * WINDOW, (j + 1) * WINDOW)
            krows = slice(j * WINDOW, (j + 2) * WINDOW)
            for kvh in range(SWA_KV_HEADS):
                p0 = kvh * (group // 2)
                q_st = jnp.concatenate([sq_ref[rows, p0 * LANES:(p0 + 1) * LANES],
                                        sq_ref[rows, (p0 + 1) * LANES:(p0 + 2) * LANES]], axis=0)
                acc = None
                for half in range(2):
                    s = _dot_nt(q_st, kk_ref[2 * kvh + half, krows, :])
                    yield
                    s_prev = s[:, 0:WINDOW]
                    if j == 0:
                        s_prev = jnp.where(i > 0, s_prev, -jnp.inf)
                    s = jnp.where(from_prev, s_prev, s[:, WINDOW:2 * WINDOW])
                    sink = jnp.where(upper, sinks_ref[layer, 2 * p0 + half],
                                     sinks_ref[layer, 2 * p0 + 2 + half]) * LOG2E
                    m = jnp.maximum(jnp.max(s, axis=-1, keepdims=True), sink)
                    e = jnp.exp2(s - m)
                    denom = jnp.sum(e, axis=-1, keepdims=True) + jnp.exp2(sink - m)
                    eb = e.astype(BF16)
                    e2 = jnp.concatenate([eb * prev_mask, eb * cur_mask], axis=1)
                    o = _dot(e2, vv_ref[2 * kvh + half, krows, :]) * (1.0 / denom)
                    acc = o if acc is None else acc + o
                    yield
                mix_ref[rows, p0 * LANES:(p0 + 1) * LANES] = acc[0:WINDOW]
                mix_ref[rows, (p0 + 1) * LANES:(p0 + 2) * LANES] = acc[WINDOW:2 * WINDOW]
            emitted.add(("swa", j))

    head_ones = ((_iota((DN_WIDTH, DN_WIDTH), 0) // HEAD_DIM)
                 == (_iota((DN_WIDTH, DN_WIDTH), 1) // HEAD_DIM)).astype(BF16)

    def conv_stream():
        for r0 in range(0, ts, ROW_BLK):
            rows = slice(r0, r0 + ROW_BLK)
            for part, dst in enumerate((qn_ref, kn_ref, vn_ref)):
                cols = slice(part * DN_WIDTH, (part + 1) * DN_WIDTH)
                assert ("dn", r0, part) in emitted
                xs = dnp_ref[r0:r0 + CONV_HIST + ROW_BLK, cols]
                acc = xs * conv_w_ref[0:1, cols]
                for j in range(1, DN_CONV):
                    acc = pltpu.roll(acc, 1, 0) + xs * conv_w_ref[j:j + 1, cols]
                conv = _silu(acc[CONV_HIST:CONV_HIST + ROW_BLK, :])
                if part == 0:
                    dst[rows, :] = conv * (lax.rsqrt(_dot(conv * conv, head_ones) + EPS) * SCALE)
                elif part == 1:
                    dst[rows, :] = conv * lax.rsqrt(_dot(conv * conv, head_ones) + EPS)
                else:
                    dst[rows, :] = conv
                yield
            emitted.add(("conv", r0))
            for _ in range(PERIOD - 3):
                yield
        dnp_ref[0:CONV_HIST, :] = dnp_ref[ts:ts + CONV_HIST, :]

    cc = _iota((DN_CHUNK, GW), 0)
    cs = _iota((DN_CHUNK, GW), 1) % HEAD_DIM
    causal = cc >= cs
    strict = cc > cs
    diag = cc == cs
    eye2 = diag.astype(F32)
    lo_g = _iota((DN_CHUNK, GW), 1) < HEAD_DIM
    tri = (_iota((DN_CHUNK, DN_CHUNK), 0) >= _iota((DN_CHUNK, DN_CHUNK), 1)).astype(BF16)
    bd_mask = ((_iota((GW, GW), 0) // HEAD_DIM) == (_iota((GW, GW), 1) // HEAD_DIM)).astype(BF16)

    def bd(x):
        xb = x.astype(BF16)
        return jnp.concatenate([xb] * HPG, axis=0) * bd_mask

    def pack_diag(x):
        return jnp.where(lo_g, x[0:HEAD_DIM], x[HEAD_DIM:2 * HEAD_DIM])

    items = [(slice(c * DN_CHUNK, (c + 1) * DN_CHUNK), slice(gidx * GW, (gidx + 1) * GW), gidx)
             for c in range(nchunk) for gidx in range(NG)]

    local = []

    def gdn_stream():
        for b0 in range(0, len(items), GDN_BATCH):
            batch = items[b0:b0 + GDN_BATCH]
            n = len(batch)
            for rows, _, _ in batch:
                rb0 = rows.start // ROW_BLK * ROW_BLK
                assert ("conv", rb0) in emitted and ("ba", rb0) in emitted
            q2 = [qn_ref[rows, lanes] for rows, lanes, _ in batch]
            k2 = [kn_ref[rows, lanes] for rows, lanes, _ in batch]
            beta2 = [bb_ref[rows, lanes] for rows, lanes, _ in batch]
            gcs = {}
            for rows, _, _ in batch:
                if rows.start not in gcs:
                    gcs[rows.start] = _sel_dot(tri, gb_ref[rows, :])
            gc = [gcs[rows.start][:, lanes] for rows, lanes, _ in batch]
            kb2 = [k2[t] * beta2[t] for t in range(n)]
            kq = [_dot_nt(jnp.concatenate([kb2[t], q2[t]], axis=0), bd(k2[t])) for t in range(n)]
            yield
            grow = [jnp.sum(jnp.where(diag, g, 0.0), axis=0, keepdims=True) for g in gc]
            dec = [jnp.exp(jnp.where(causal, gc[t] - grow[t], -jnp.inf)) for t in range(n)]
            nmat = [jnp.where(strict, -(kq[t][0:DN_CHUNK] * dec[t]), 0.0) for t in range(n)]
            amat = [kq[t][DN_CHUNK:2 * DN_CHUNK] * dec[t] for t in range(n)]
            xinv = [eye2 + m for m in nmat]
            pw = [_dot(m, bd(m)) for m in nmat]
            yield
            for r in range(1, 6):
                if r < 5:
                    res = [_dot(pw[t], jnp.concatenate([bd(pw[t]), bd(xinv[t])], axis=1)) for t in range(n)]
                    pw = [x[:, 0:GW] for x in res]
                    xinv = [xinv[t] + res[t][:, GW:2 * GW] for t in range(n)]
                else:
                    xinv = [xinv[t] + _dot(pw[t], bd(xinv[t])) for t in range(n)]
                yield
            eg = [jnp.exp(g) for g in gc]
            rhs = []
            for t in range(n):
                rows, lanes, _ = batch[t]
                rhs.append(jnp.concatenate([bd(vn_ref[rows, lanes] * beta2[t]), bd(kb2[t] * eg[t])], axis=1))
            uw = [_dot(xinv[t], rhs[t]) for t in range(n)]
            yield
            glast = [g[DN_CHUNK - 1:DN_CHUNK, :] for g in gc]
            kuw = [_dot_tn(k2[t] * jnp.exp(glast[t] - gc[t]), uw[t]) for t in range(n)]
            auw = [_dot(amat[t], jnp.concatenate([bd(uw[t][:, 0:GW]), bd(uw[t][:, GW:2 * GW])], axis=1))
                   for t in range(n)]
            yield
            for t in range(n):
                rows, lanes, gidx = batch[t]
                m2 = eye2 * jnp.exp(glast[t]) - pack_diag(kuw[t][:, GW:2 * GW])
                r2 = pack_diag(kuw[t][:, 0:GW])
                qeff = q2[t] * eg[t] - auw[t][:, GW:2 * GW]
                local.append((rows, lanes, gidx, m2, r2, qeff, auw[t][:, 0:GW]))
            emitted.add(("gdn", b0))
            for _ in range(PERIOD - GDN_STEPS + 1):
                yield

    def chain_stream():
        states = [s_ref[gidx] for gidx in range(NG)]
        for b0 in range(0, len(items), GDN_BATCH):
            assert ("gdn", b0) in emitted
            outs = []
            for rows, lanes, gidx, m2, r2, qeff, oloc in local[b0:b0 + GDN_BATCH]:
                res = _dot(jnp.concatenate([qeff, m2], axis=0), bd(states[gidx]))
                outs.append(oloc + res[0:DN_CHUNK])
                states[gidx] = res[DN_CHUNK:2 * DN_CHUNK] + r2
                if gidx == NG - 1:
                    od = jnp.concatenate(outs, axis=1)
                    outs = []
                    mix_ref[rows, SWA_WIDTH:SWA_WIDTH + DN_WIDTH] = (
                        od * lax.rsqrt(_dot(od * od, head_ones) * (1.0 / HEAD_DIM) + EPS) * dn_g_ref[...])
                    emitted.add(("chain", rows.start))
                    yield
            for _ in range(PERIOD - CHAIN_STEPS):
                yield
        for gidx in range(NG):
            s_ref[gidx] = states[gidx]

    def mem_stream():
        for r0 in range(0, ts, MEM_ROWS):
            rows = slice(r0, r0 + MEM_ROWS)
            assert ("mq", r0) in emitted
            for p in range(MEM_HEADS // 2):
                q2 = mq_ref[rows, p * LANES:(p + 1) * LANES]
                acc = None
                for half in range(2):
                    s = _dot_nt(q2, mkv_ref[p * 2 + half])
                    yield
                    e = jnp.exp2(s - jnp.max(s, axis=-1, keepdims=True))
                    o = _dot(e, mkv_ref[4 + p * 2 + half]) * (1.0 / jnp.sum(e, axis=-1, keepdims=True))
                    acc = o if acc is None else acc + o
                    yield
                off = SWA_WIDTH + DN_WIDTH + p * LANES
                mix_ref[rows, off:off + LANES] = acc
            emitted.add(("mem", r0))
            for _ in range(PERIOD - MEM_STEPS):
                yield

    def out_stream(r0):
        rows = slice(r0, r0 + ROW_BLK)
        assert ("gate", r0) in emitted and ("mem", r0) in emitted
        assert all(("swa", j) in emitted for j in range(r0 // WINDOW, (r0 + ROW_BLK) // WINDOW))
        assert all(("chain", c0) in emitted for c0 in range(r0, r0 + ROW_BLK, DN_CHUNK))
        mixed = (mix_ref[rows, :] * _silu(gate_ref[rows, :])).astype(BF16)
        yield
        y = jnp.dot(mixed, w_out_ref[...], preferred_element_type=F32)
        yield
        out_ref[0, rows, :] = x_ref[0, rows, :] + _rms(y, post_g_ref[...])

    blk_per_rb = ROW_BLK // WINDOW
    nrb = ts // ROW_BLK
    swa_streams = [_delayed(SWA_DELAY + rb * PERIOD, swa_stream(range(rb * blk_per_rb, (rb + 1) * blk_per_rb)))
                   for rb in range(nrb)]
    out_streams = [_delayed(OUT_DELAY + rb * PERIOD, out_stream(rb * ROW_BLK)) for rb in range(nrb)]
    _round_robin([proj_stream(), _delayed(CONV_DELAY, conv_stream())] + swa_streams
                 + [_delayed(GDN_DELAY, gdn_stream()), _delayed(MEM_DELAY, mem_stream()),
                    _delayed(CHAIN_DELAY, chain_stream())] + out_streams)
    for ref in (kk_ref, vv_ref):
        ref[:, 0:WINDOW, :] = ref[:, ts:ts + WINDOW, :]


def _layer(layer, x, mem, sinks, pre_g, w_in, w_ba, conv_w, alog, dtb, dn_g, mem_g, w_mem, w_out, post_g):
    b, s, d = x.shape
    ts = min(SEQ_TILE, s)
    assert s % ts == 0 and ts % ROW_BLK == 0 and d == D_MODEL
    this_layer = lambda bi, si: (layer, 0, 0)
    resident = pl.Buffered(1)
    return pl.pallas_call(
        functools.partial(_layer_kernel, ts=ts, layer=layer),
        out_shape=jax.ShapeDtypeStruct(x.shape, x.dtype),
        grid=(b, s // ts),
        in_specs=[
            pl.BlockSpec(memory_space=pltpu.SMEM),
            pl.BlockSpec((1, ts, d), lambda bi, si: (bi, si, 0)),
            pl.BlockSpec((1, N_MEM, d), lambda bi, si: (bi, 0, 0)),
            pl.BlockSpec((None, 1, d), this_layer),
            pl.BlockSpec((None, d, IN_COLS), this_layer, pipeline_mode=resident),
            pl.BlockSpec((None, BA_ROWS, d), this_layer),
            pl.BlockSpec((None, DN_CONV, 3 * DN_WIDTH), this_layer),
            pl.BlockSpec((None, BA_ROWS, 1), this_layer),
            pl.BlockSpec((None, BA_ROWS, 1), this_layer),
            pl.BlockSpec((None, 1, DN_WIDTH), this_layer),
            pl.BlockSpec((None, 1, d), this_layer),
            pl.BlockSpec((None, d, 2 * MEM_WIDTH), this_layer, pipeline_mode=resident),
            pl.BlockSpec((None, MIX_WIDTH, d), this_layer, pipeline_mode=resident),
            pl.BlockSpec((None, 1, d), this_layer),
        ],
        out_specs=pl.BlockSpec((1, ts, d), lambda bi, si: (bi, si, 0)),
        scratch_shapes=[
            pltpu.VMEM((ts, d), BF16),
            pltpu.VMEM((ts, SWA_WIDTH), BF16),
            pltpu.VMEM((4, WINDOW + ts, LANES), BF16),
            pltpu.VMEM((4, WINDOW + ts, LANES), BF16),
            pltpu.VMEM((CONV_HIST + ts, 3 * DN_WIDTH), F32),
            pltpu.VMEM((ts, DN_WIDTH), F32),
            pltpu.VMEM((ts, DN_WIDTH), F32),
            pltpu.VMEM((ts, DN_WIDTH), F32),
            pltpu.VMEM((ts, DN_WIDTH), F32),
            pltpu.VMEM((ts, DN_WIDTH), F32),
            pltpu.VMEM((NG, DN_CHUNK, GW), F32),
            pltpu.VMEM((8, N_MEM, LANES), BF16),
            pltpu.VMEM((ts, MEM_WIDTH), BF16),
            pltpu.VMEM((ts, MIX_WIDTH), F32),
            pltpu.VMEM((ts, MIX_WIDTH), F32),
        ],
        compiler_params=pltpu.CompilerParams(
            dimension_semantics=("arbitrary", "arbitrary"),
            vmem_limit_bytes=VMEM_LIMIT_BYTES),
        name="hybrid_layer",
    )(sinks, x, mem, pre_g, w_in, w_ba, conv_w, alog, dtb, dn_g, mem_g, w_mem, w_out, post_g)


_BA_COL = SWA_WIDTH + 2 * LANES + 3 * DN_WIDTH
_AFTER_BA = _BA_COL + 2 * DN_HEADS


def _prep_w_in(w_in):
    main = jnp.concatenate([w_in[:, :, 0:_BA_COL], w_in[:, :, _AFTER_BA:]], axis=2)
    col = lax.broadcasted_iota(jnp.int32, (1, 1, IN_COLS), 2)
    is_query = (col < SWA_WIDTH) | ((col >= C_MQ) & (col < C_MQ + MEM_WIDTH))
    main = main * jnp.where(is_query, SCALE * LOG2E, 1.0)
    w_ba = jnp.swapaxes(w_in[:, :, _BA_COL:_AFTER_BA], 1, 2)
    w_ba = jnp.pad(w_ba, ((0, 0), (0, BA_ROWS - 2 * DN_HEADS), (0, 0)))
    return main.astype(BF16), w_ba.astype(BF16)


def _row_place(v, start):
    return jnp.pad(v.astype(F32), ((0, 0), (start, BA_ROWS - start - v.shape[1])))[:, :, None]


@jax.jit
def kernel(x, mem, pre_norm_g, w_in, conv_w, a_log, dt_bias, sinks, dn_norm_g, mem_norm_g, w_mem_kv,
           w_out, post_norm_g):
    w_main, w_ba = _prep_w_in(w_in)
    params = (sinks.astype(F32), pre_norm_g[:, None, :], w_main, w_ba, conv_w,
              _row_place(a_log, DN_HEADS), _row_place(dt_bias, DN_HEADS),
              jnp.tile(dn_norm_g, (1, DN_HEADS))[:, None, :], mem_norm_g[:, None, :],
              w_mem_kv.astype(BF16), w_out.astype(BF16), post_norm_g[:, None, :])
    for layer in range(w_in.shape[0]):
        x = _layer(layer, x, mem, *params)
    return x
```

```python
import functools

import jax
import jax.numpy as jnp
from jax import lax
from jax.experimental import pallas as pl
from jax.experimental.pallas import tpu as pltpu

F32 = jnp.float32
BF16 = jnp.bfloat16

D_MODEL = 1024
HEAD_DIM = 64
SWA_HEADS = 8
SWA_KV_HEADS = 2
SWA_WIDTH = SWA_HEADS * HEAD_DIM
WINDOW = 128
DN_HEADS = 4
DN_WIDTH = DN_HEADS * HEAD_DIM
DN_CONV = 4
DN_CHUNK = 64
N_MEM = 256
MEM_HEADS = 4
MEM_WIDTH = MEM_HEADS * HEAD_DIM
MIX_WIDTH = 1024
EPS = 1e-6
SCALE = HEAD_DIM ** -0.5
LOG2E = 1.4426950408889634

LANES = 128
MXU_N = 256
CONV_HIST = 8

C_SWA = 0
C_DN = 768
C_MQ = 1536
C_GATE = 1792
IN_COLS = 2816
BA_ROWS = 16

GW = 128
NG = DN_WIDTH // GW
HPG = GW // HEAD_DIM

SEQ_TILE = 1024
GDN_BATCH = 8
ROW_BLK = 256
MEM_ROWS = ROW_BLK
PERIOD = 12
CONV_DELAY = 3
SWA_DELAY = 6
GDN_DELAY = 7
GDN_STEPS = 10
MEM_DELAY = 10
MEM_STEPS = 8
CHAIN_DELAY = GDN_DELAY + GDN_STEPS
CHAIN_STEPS = ROW_BLK // DN_CHUNK
OUT_DELAY = 22
VMEM_LIMIT_BYTES = 60 * 1024 * 1024


def _dot(a, b):
    return jnp.dot(a.astype(BF16), b.astype(BF16), preferred_element_type=F32)


def _dot_nt(a, b):
    return lax.dot_general(a.astype(BF16), b.astype(BF16), (((1,), (1,)), ((), ())),
                           preferred_element_type=F32)


def _dot_tn(a, b):
    return lax.dot_general(a.astype(BF16), b.astype(BF16), (((0,), (0,)), ((), ())),
                           preferred_element_type=F32)


def _split2(x):
    x0 = x.astype(BF16)
    x1 = (x - x0.astype(F32)).astype(BF16)
    return x0, x1


def _sel_dot(m, x):
    x0, x1 = _split2(x)
    return jnp.dot(m, x0, preferred_element_type=F32) + jnp.dot(m, x1, preferred_element_type=F32)


def _iota(shape, dim):
    return lax.broadcasted_iota(jnp.int32, shape, dim)


def _rms(x, g):
    return x * lax.rsqrt(jnp.mean(x * x, axis=-1, keepdims=True) + EPS) * g


def _silu(x):
    hx = 0.5 * x
    return hx + hx * jnp.tanh(hx)


def _round_robin(streams):
    streams = list(streams)
    while streams:
        for s in list(streams):
            try:
                next(s)
            except StopIteration:
                streams.remove(s)


def _delayed(n, gen):
    for _ in range(n):
        yield
    yield from gen


def _layer_kernel(sinks_ref, x_ref, mem_ref, pre_g_ref, w_in_ref, w_ba_ref, conv_w_ref, alog_ref, dtb_ref,
                  dn_g_ref, mem_g_ref, w_mem_ref, w_out_ref, post_g_ref, out_ref,
                  h_ref, sq_ref, kk_ref, vv_ref, dnp_ref, qn_ref, kn_ref, vn_ref, gb_ref, bb_ref,
                  s_ref, mkv_ref, mq_ref, gate_ref, mix_ref, *, ts, layer):
    i = pl.program_id(1)
    nblk = ts // WINDOW
    nchunk = ts // DN_CHUNK

    lane = _iota((1, LANES), 1)
    lo = lane < HEAD_DIM
    emitted = set()

    @pl.when(i == 0)
    def _():
        s_ref[...] = jnp.zeros_like(s_ref)
        kk_ref[:, 0:WINDOW, :] = jnp.zeros((4, WINDOW, LANES), BF16)
        vv_ref[:, 0:WINDOW, :] = jnp.zeros((4, WINDOW, LANES), BF16)
        dnp_ref[0:CONV_HIST, :] = jnp.zeros((CONV_HIST, 3 * DN_WIDTH), F32)
        hm = _rms(mem_ref[0], mem_g_ref[...])
        mkv = _dot(hm, w_mem_ref[...])
        for kind in range(2):
            for p in range(MEM_HEADS // 2):
                blk = mkv[:, kind * MEM_WIDTH + p * LANES: kind * MEM_WIDTH + (p + 1) * LANES]
                mkv_ref[kind * 4 + p * 2 + 0] = jnp.where(lo, blk, 0.0).astype(BF16)
                mkv_ref[kind * 4 + p * 2 + 1] = jnp.where(lo, 0.0, blk).astype(BF16)

    h_ref[...] = _rms(x_ref[0], pre_g_ref[...]).astype(BF16)

    def proj(c0, width, rows):
        return jnp.dot(h_ref[rows, :], w_in_ref[:, c0:c0 + width], preferred_element_type=F32)

    sel_r = _iota((BA_ROWS, DN_WIDTH), 0)
    sel_c = _iota((BA_ROWS, DN_WIDTH), 1) // HEAD_DIM
    beta_sel = (sel_r == sel_c).astype(BF16)
    decay_sel = (sel_r == DN_HEADS + sel_c).astype(BF16)

    def proj_stream():
        for r0 in range(0, ts, ROW_BLK):
            rows = slice(r0, r0 + ROW_BLK)
            hist_rows = slice(CONV_HIST + r0, CONV_HIST + r0 + ROW_BLK)
            kv_rows = slice(WINDOW + r0, WINDOW + r0 + ROW_BLK)
            for c in range(3 * DN_WIDTH // MXU_N):
                dnp_ref[hist_rows, c * MXU_N:(c + 1) * MXU_N] = proj(C_DN + c * MXU_N, MXU_N, rows)
                emitted.add(("dn", r0, c))
                yield
            for c in range(SWA_WIDTH // MXU_N):
                sq_ref[rows, c * MXU_N:(c + 1) * MXU_N] = proj(C_SWA + c * MXU_N, MXU_N, rows).astype(BF16)
                yield
            kvcur = proj(C_SWA + SWA_WIDTH, 2 * LANES, rows)
            for ref, cur in ((kk_ref, kvcur[:, 0:LANES]), (vv_ref, kvcur[:, LANES:2 * LANES])):
                rolled = pltpu.roll(cur, HEAD_DIM, 1)
                ref[0, kv_rows, :] = jnp.where(lo, cur, 0.0).astype(BF16)
                ref[1, kv_rows, :] = jnp.where(lo, 0.0, rolled).astype(BF16)
                ref[2, kv_rows, :] = jnp.where(lo, rolled, 0.0).astype(BF16)
                ref[3, kv_rows, :] = jnp.where(lo, 0.0, cur).astype(BF16)
            emitted.add(("qkv", r0))
            yield
            ba = _dot_nt(w_ba_ref[...], h_ref[rows, :])
            beta = 1.0 / (1.0 + jnp.exp(-ba))
            z = ba + dtb_ref[...]
            softplus = jnp.maximum(z, 0.0) + jnp.log(1.0 + jnp.exp(-jnp.abs(z)))
            glog = -jnp.exp(alog_ref[...]) * softplus
            bb_ref[rows, :] = _dot_tn(beta, beta_sel)
            g0, g1 = _split2(glog)
            gb_ref[rows, :] = _dot_tn(g0, decay_sel) + _dot_tn(g1, decay_sel)
            emitted.add(("ba", r0))
            yield
            mq_ref[rows, :] = proj(C_MQ, MEM_WIDTH, rows).astype(BF16)
            emitted.add(("mq", r0))
            yield
            for c in range(MIX_WIDTH // MXU_N):
                gate_ref[rows, c * MXU_N:(c + 1) * MXU_N] = proj(C_GATE + c * MXU_N, MXU_N, rows).astype(BF16)
                if c == MIX_WIDTH // MXU_N - 1:
                    emitted.add(("gate", r0))
                yield

    from_prev = _iota((2 * WINDOW, WINDOW), 1) > (_iota((2 * WINDOW, WINDOW), 0) % WINDOW)
    prev_mask = from_prev.astype(BF16)
    cur_mask = 1.0 - prev_mask
    upper = _iota((2 * WINDOW, 1), 0) < WINDOW
    group = SWA_HEADS // SWA_KV_HEADS

    def swa_stream(blocks):
        for j in blocks:
            assert ("qkv", j * WINDOW // ROW_BLK * ROW_BLK) in emitted
            rows = slice(j * WINDOW, (j + 1) * WINDOW)
            krows = slice(j * WINDOW, (j + 2) * WINDOW)
            for kvh in range(SWA_KV_HEADS):
                p0 = kvh * (group // 2)
                q_st = jnp.concatenate([sq_ref[rows, p0 * LANES:(p0 + 1) * LANES],
                                        sq_ref[rows, (p0 + 1) * LANES:(p0 + 2) * LANES]], axis=0)
                acc = None
                for half in range(2):
                    s = _dot_nt(q_st, kk_ref[2 * kvh + half, krows, :])
                    yield
                    s_prev = s[:, 0:WINDOW]
                    if j == 0:
                        s_prev = jnp.where(i > 0, s_prev, -jnp.inf)
                    s = jnp.where(from_prev, s_prev, s[:, WINDOW:2 * WINDOW])
                    sink = jnp.where(upper, sinks_ref[layer, 2 * p0 + half],
                                     sinks_ref[layer, 2 * p0 + 2 + half]) * LOG2E
                    m = jnp.maximum(jnp.max(s, axis=-1, keepdims=True), sink)
                    e = jnp.exp2(s - m)
                    denom = jnp.sum(e, axis=-1, keepdims=True) + jnp.exp2(sink - m)
                    eb = e.astype(BF16)
                    e2 = jnp.concatenate([eb * prev_mask, eb * cur_mask], axis=1)
                    o = _dot(e2, vv_ref[2 * kvh + half, krows, :]) * (1.0 / denom)
                    acc = o if acc is None else acc + o
                    yield
                mix_ref[rows, p0 * LANES:(p0 + 1) * LANES] = acc[0:WINDOW]
                mix_ref[rows, (p0 + 1) * LANES:(p0 + 2) * LANES] = acc[WINDOW:2 * WINDOW]
            emitted.add(("swa", j))

    head_ones = ((_iota((DN_WIDTH, DN_WIDTH), 0) // HEAD_DIM)
                 == (_iota((DN_WIDTH, DN_WIDTH), 1) // HEAD_DIM)).astype(BF16)

    def conv_stream():
        for r0 in range(0, ts, ROW_BLK):
            rows = slice(r0, r0 + ROW_BLK)
            for part, dst in enumerate((qn_ref, kn_ref, vn_ref)):
                cols = slice(part * DN_WIDTH, (part + 1) * DN_WIDTH)
                assert ("dn", r0, part) in emitted
                xs = dnp_ref[r0:r0 + CONV_HIST + ROW_BLK, cols]
                acc = xs * conv_w_ref[0:1, cols]
                for j in range(1, DN_CONV):
                    acc = pltpu.roll(acc, 1, 0) + xs * conv_w_ref[j:j + 1, cols]
                conv = _silu(acc[CONV_HIST:CONV_HIST + ROW_BLK, :])
                if part == 0:
                    dst[rows, :] = conv * (lax.rsqrt(_dot(conv * conv, head_ones) + EPS) * SCALE)
                elif part == 1:
                    dst[rows, :] = conv * lax.rsqrt(_dot(conv * conv, head_ones) + EPS)
                else:
                    dst[rows, :] = conv
                yield
            emitted.add(("conv", r0))
            for _ in range(PERIOD - 3):
                yield
        dnp_ref[0:CONV_HIST, :] = dnp_ref[ts:ts + CONV_HIST, :]

    cc = _iota((DN_CHUNK, GW), 0)
    cs = _iota((DN_CHUNK, GW), 1) % HEAD_DIM
    causal = cc >= cs
    strict = cc > cs
    diag = cc == cs
    eye2 = diag.astype(F32)
    lo_g = _iota((DN_CHUNK, GW), 1) < HEAD_DIM
    tri = (_iota((DN_CHUNK, DN_CHUNK), 0) >= _iota((DN_CHUNK, DN_CHUNK), 1)).astype(BF16)
    bd_mask = ((_iota((GW, GW), 0) // HEAD_DIM) == (_iota((GW, GW), 1) // HEAD_DIM)).astype(BF16)

    def bd(x):
        xb = x.astype(BF16)
        return jnp.concatenate([xb] * HPG, axis=0) * bd_mask

    def pack_diag(x):
        return jnp.where(lo_g, x[0:HEAD_DIM], x[HEAD_DIM:2 * HEAD_DIM])

    items = [(slice(c * DN_CHUNK, (c + 1) * DN_CHUNK), slice(gidx * GW, (gidx + 1) * GW), gidx)
             for c in range(nchunk) for gidx in range(NG)]

    local = []

    def gdn_stream():
        for b0 in range(0, len(items), GDN_BATCH):
            batch = items[b0:b0 + GDN_BATCH]
            n = len(batch)
            for rows, _, _ in batch:
                rb0 = rows.start // ROW_BLK * ROW_BLK
                assert ("conv", rb0) in emitted and ("ba", rb0) in emitted
            q2 = [qn_ref[rows, lanes] for rows, lanes, _ in batch]
            k2 = [kn_ref[rows, lanes] for rows, lanes, _ in batch]
            beta2 = [bb_ref[rows, lanes] for rows, lanes, _ in batch]
            gcs = {}
            for rows, _, _ in batch:
                if rows.start not in gcs:
                    gcs[rows.start] = _sel_dot(tri, gb_ref[rows, :])
            gc = [gcs[rows.start][:, lanes] for rows, lanes, _ in batch]
            kb2 = [k2[t] * beta2[t] for t in range(n)]
            kq = [_dot_nt(jnp.concatenate([kb2[t], q2[t]], axis=0), bd(k2[t])) for t in range(n)]
            yield
            grow = [jnp.sum(jnp.where(diag, g, 0.0), axis=0, keepdims=True) for g in gc]
            dec = [jnp.exp(jnp.where(causal, gc[t] - grow[t], -jnp.inf)) for t in range(n)]
            nmat = [jnp.where(strict, -(kq[t][0:DN_CHUNK] * dec[t]), 0.0) for t in range(n)]
            amat = [kq[t][DN_CHUNK:2 * DN_CHUNK] * dec[t] for t in range(n)]
            xinv = [eye2 + m for m in nmat]
            pw = [_dot(m, bd(m)) for m in nmat]
            yield
            for r in range(1, 6):
                if r < 5:
                    res = [_dot(pw[t], jnp.concatenate([bd(pw[t]), bd(xinv[t])], axis=1)) for t in range(n)]
                    pw = [x[:, 0:GW] for x in res]
                    xinv = [xinv[t] + res[t][:, GW:2 * GW] for t in range(n)]
                else:
                    xinv = [xinv[t] + _dot(pw[t], bd(xinv[t])) for t in range(n)]
                yield
            eg = [jnp.exp(g) for g in gc]
            rhs = []
            for t in range(n):
                rows, lanes, _ = batch[t]
                rhs.append(jnp.concatenate([bd(vn_ref[rows, lanes] * beta2[t]), bd(kb2[t] * eg[t])], axis=1))
            uw = [_dot(xinv[t], rhs[t]) for t in range(n)]
            yield
            glast = [g[DN_CHUNK - 1:DN_CHUNK, :] for g in gc]
            kuw = [_dot_tn(k2[t] * jnp.exp(glast[t] - gc[t]), uw[t]) for t in range(n)]
            auw = [_dot(amat[t], jnp.concatenate([bd(uw[t][:, 0:GW]), bd(uw[t][:, GW:2 * GW])], axis=1))
                   for t in range(n)]
            yield
            for t in range(n):
                rows, lanes, gidx = batch[t]
                m2 = eye2 * jnp.exp(glast[t]) - pack_diag(kuw[t][:, GW:2 * GW])
                r2 = pack_diag(kuw[t][:, 0:GW])
                qeff = q2[t] * eg[t] - auw[t][:, GW:2 * GW]
                local.append((rows, lanes, gidx, m2, r2, qeff, auw[t][:, 0:GW]))
            emitted.add(("gdn", b0))
            for _ in range(PERIOD - GDN_STEPS + 1):
                yield

    def chain_stream():
        states = [s_ref[gidx] for gidx in range(NG)]
        for b0 in range(0, len(items), GDN_BATCH):
            assert ("gdn", b0) in emitted
            outs = []
            for rows, lanes, gidx, m2, r2, qeff, oloc in local[b0:b0 + GDN_BATCH]:
                res = _dot(jnp.concatenate([qeff, m2], axis=0), bd(states[gidx]))
                outs.append(oloc + res[0:DN_CHUNK])
                states[gidx] = res[DN_CHUNK:2 * DN_CHUNK] + r2
                if gidx == NG - 1:
                    od = jnp.concatenate(outs, axis=1)
                    outs = []
                    mix_ref[rows, SWA_WIDTH:SWA_WIDTH + DN_WIDTH] = (
                        od * lax.rsqrt(_dot(od * od, head_ones) * (1.0 / HEAD_DIM) + EPS) * dn_g_ref[...])
                    emitted.add(("chain", rows.start))
                    yield
            for _ in range(PERIOD - CHAIN_STEPS):
                yield
        for gidx in range(NG):
            s_ref[gidx] = states[gidx]

    def mem_stream():
        for r0 in range(0, ts, MEM_ROWS):
            rows = slice(r0, r0 + MEM_ROWS)
            assert ("mq", r0) in emitted
            for p in range(MEM_HEADS // 2):
                q2 = mq_ref[rows, p * LANES:(p + 1) * LANES]
                acc = None
                for half in range(2):
                    s = _dot_nt(q2, mkv_ref[p * 2 + half])
                    yield
                    e = jnp.exp2(s - jnp.max(s, axis=-1, keepdims=True))
                    o = _dot(e, mkv_ref[4 + p * 2 + half]) * (1.0 / jnp.sum(e, axis=-1, keepdims=True))
                    acc = o if acc is None else acc + o
                    yield
                off = SWA_WIDTH + DN_WIDTH + p * LANES
                mix_ref[rows, off:off + LANES] = acc
            emitted.add(("mem", r0))
            for _ in range(PERIOD - MEM_STEPS):
                yield

    def out_stream(r0):
        rows = slice(r0, r0 + ROW_BLK)
        assert ("gate", r0) in emitted and ("mem", r0) in emitted
        assert all(("swa", j) in emitted for j in range(r0 // WINDOW, (r0 + ROW_BLK) // WINDOW))
        assert all(("chain", c0) in emitted for c0 in range(r0, r0 + ROW_BLK, DN_CHUNK))
        mixed = mix_ref[rows, :].astype(BF16) * _silu(gate_ref[rows, :])
        yield
        y = jnp.dot(mixed, w_out_ref[...], preferred_element_type=F32)
        yield
        out_ref[0, rows, :] = x_ref[0, rows, :] + _rms(y, post_g_ref[...])

    blk_per_rb = ROW_BLK // WINDOW
    nrb = ts // ROW_BLK
    swa_streams = [_delayed(SWA_DELAY + rb * PERIOD, swa_stream(range(rb * blk_per_rb, (rb + 1) * blk_per_rb)))
                   for rb in range(nrb)]
    out_streams = [_delayed(OUT_DELAY + rb * PERIOD, out_stream(rb * ROW_BLK)) for rb in range(nrb)]
    _round_robin([proj_stream(), _delayed(CONV_DELAY, conv_stream())] + swa_streams
                 + [_delayed(GDN_DELAY, gdn_stream()), _delayed(MEM_DELAY, mem_stream()),
                    _delayed(CHAIN_DELAY, chain_stream())] + out_streams)
    for ref in (kk_ref, vv_ref):
        ref[:, 0:WINDOW, :] = ref[:, ts:ts + WINDOW, :]


def _layer(layer, x, mem, sinks, pre_g, w_in, w_ba, conv_w, alog, dtb, dn_g, mem_g, w_mem, w_out, post_g):
    b, s, d = x.shape
    ts = min(SEQ_TILE, s)
    assert s % ts == 0 and ts % ROW_BLK == 0 and d == D_MODEL
    this_layer = lambda bi, si: (layer, 0, 0)
    resident = pl.Buffered(1)
    return pl.pallas_call(
        functools.partial(_layer_kernel, ts=ts, layer=layer),
        out_shape=jax.ShapeDtypeStruct(x.shape, x.dtype),
        grid=(b, s // ts),
        in_specs=[
            pl.BlockSpec(memory_space=pltpu.SMEM),
            pl.BlockSpec((1, ts, d), lambda bi, si: (bi, si, 0)),
            pl.BlockSpec((1, N_MEM, d), lambda bi, si: (bi, 0, 0)),
            pl.BlockSpec((None, 1, d), this_layer),
            pl.BlockSpec((None, d, IN_COLS), this_layer, pipeline_mode=resident),
            pl.BlockSpec((None, BA_ROWS, d), this_layer),
            pl.BlockSpec((None, DN_CONV, 3 * DN_WIDTH), this_layer),
            pl.BlockSpec((None, BA_ROWS, 1), this_layer),
            pl.BlockSpec((None, BA_ROWS, 1), this_layer),
            pl.BlockSpec((None, 1, DN_WIDTH), this_layer),
            pl.BlockSpec((None, 1, d), this_layer),
            pl.BlockSpec((None, d, 2 * MEM_WIDTH), this_layer, pipeline_mode=resident),
            pl.BlockSpec((None, MIX_WIDTH, d), this_layer, pipeline_mode=resident),
            pl.BlockSpec((None, 1, d), this_layer),
        ],
        out_specs=pl.BlockSpec((1, ts, d), lambda bi, si: (bi, si, 0)),
        scratch_shapes=[
            pltpu.VMEM((ts, d), BF16),
            pltpu.VMEM((ts, SWA_WIDTH), BF16),
            pltpu.VMEM((4, WINDOW + ts, LANES), BF16),
            pltpu.VMEM((4, WINDOW + ts, LANES), BF16),
            pltpu.VMEM((CONV_HIST + ts, 3 * DN_WIDTH), F32),
            pltpu.VMEM((ts, DN_WIDTH), F32),
            pltpu.VMEM((ts, DN_WIDTH), F32),
            pltpu.VMEM((ts, DN_WIDTH), F32),
            pltpu.VMEM((ts, DN_WIDTH), F32),
            pltpu.VMEM((ts, DN_WIDTH), F32),
            pltpu.VMEM((NG, DN_CHUNK, GW), F32),
            pltpu.VMEM((8, N_MEM, LANES), BF16),
            pltpu.VMEM((ts, MEM_WIDTH), BF16),
            pltpu.VMEM((ts, MIX_WIDTH), BF16),
            pltpu.VMEM((ts, MIX_WIDTH), F32),
        ],
        compiler_params=pltpu.CompilerParams(
            dimension_semantics=("arbitrary", "arbitrary"),
            vmem_limit_bytes=VMEM_LIMIT_BYTES),
        name="hybrid_layer",
    )(sinks, x, mem, pre_g, w_in, w_ba, conv_w, alog, dtb, dn_g, mem_g, w_mem, w_out, post_g)


_BA_COL = SWA_WIDTH + 2 * LANES + 3 * DN_WIDTH
_AFTER_BA = _BA_COL + 2 * DN_HEADS


def _prep_w_in(w_in):
    main = jnp.concatenate([w_in[:, :, 0:_BA_COL], w_in[:, :, _AFTER_BA:]], axis=2)
    col = lax.broadcasted_iota(jnp.int32, (1, 1, IN_COLS), 2)
    is_query = (col < SWA_WIDTH) | ((col >= C_MQ) & (col < C_MQ + MEM_WIDTH))
    main = main * jnp.where(is_query, SCALE * LOG2E, 1.0)
    w_ba = jnp.swapaxes(w_in[:, :, _BA_COL:_AFTER_BA], 1, 2)
    w_ba = jnp.pad(w_ba, ((0, 0), (0, BA_ROWS - 2 * DN_HEADS), (0, 0)))
    return main.astype(BF16), w_ba.astype(BF16)


def _row_place(v, start):
    return jnp.pad(v.astype(F32), ((0, 0), (start, BA_ROWS - start - v.shape[1])))[:, :, None]


@jax.jit
def kernel(x, mem, pre_norm_g, w_in, conv_w, a_log, dt_bias, sinks, dn_norm_g, mem_norm_g, w_mem_kv,
           w_out, post_norm_g):
    w_main, w_ba = _prep_w_in(w_in)
    params = (sinks.astype(F32), pre_norm_g[:, None, :], w_main, w_ba, conv_w,
              _row_place(a_log, DN_HEADS), _row_place(dt_bias, DN_HEADS),
              jnp.tile(dn_norm_g, (1, DN_HEADS))[:, None, :], mem_norm_g[:, None, :],
              w_mem_kv.astype(BF16), w_out.astype(BF16), post_norm_g[:, None, :])
    for layer in range(w_in.shape[0]):
        x = _layer(layer, x, mem, *params)
    return x
```
